```python
import jax, jax.numpy as jnp
from jax import lax
import numpy as np

D_MODEL = 2048
BATCH = 2
SEQ = 8192
DEPTH = 4

POOL_WIDTH = D_MODEL // 4
POOL_WINDOWS = (2, 4, 8, 16)
POOL_GROUP = POOL_WIDTH // len(POOL_WINDOWS)
CONV_WIDTH = D_MODEL // 4
CONV_KERNEL = 31
RET_WIDTH = D_MODEL // 2
RET_HEADS = 8
RET_HEAD_DIM = RET_WIDTH // RET_HEADS
RET_CHUNK = 128
ROPE_BASE = 10000.0
N_BRANCH = 3
FFN_HIDDEN = -(-8 * D_MODEL // (3 * 256)) * 256
NORM_EPS = 1e-6
LN_EPS = 1e-5

IN_SPLITS = (POOL_WIDTH, 2 * CONV_WIDTH, RET_WIDTH, RET_WIDTH, RET_WIDTH, RET_WIDTH, N_BRANCH * D_MODEL)
N_IN = sum(IN_SPLITS)
IN_OFFSETS = tuple(int(v) for v in np.cumsum(IN_SPLITS)[:-1])

kernel_name = "hybrid_pool_conformer_retention_gated"


def rms_norm(x, g):
    xf = x.astype(jnp.float32)
    y = xf * lax.rsqrt(jnp.mean(xf * xf, axis=-1, keepdims=True) + NORM_EPS)
    return (y * g.astype(jnp.float32)).astype(x.dtype)


def layer_norm(x, g, b):
    xf = x.astype(jnp.float32)
    mu = jnp.mean(xf, axis=-1, keepdims=True)
    var = jnp.mean(jnp.square(xf - mu), axis=-1, keepdims=True)
    y = (xf - mu) * lax.rsqrt(var + LN_EPS)
    return (y * g.astype(jnp.float32) + b.astype(jnp.float32)).astype(x.dtype)


def pool_mixer(u, pool_w, pool_scale):
    B, S, _ = u.shape
    uf = u.astype(jnp.float32)
    csum = jnp.cumsum(uf, axis=1)
    cpad = jnp.concatenate([jnp.zeros((B, 1, POOL_WIDTH), jnp.float32), csum], axis=1)
    t = jnp.arange(S)
    groups = []
    for gi, w in enumerate(POOL_WINDOWS):
        sl = slice(gi * POOL_GROUP, (gi + 1) * POOL_GROUP)
        hi = cpad[:, 1:, sl]
        lo = jnp.concatenate([jnp.zeros((B, w - 1, POOL_GROUP), jnp.float32),
                              cpad[:, :S - w + 1, sl]], axis=1)
        cnt = jnp.minimum(t + 1, w).astype(jnp.float32)[None, :, None]
        groups.append((hi - lo) / cnt - uf[:, :, sl])
    p = jnp.stack(groups, axis=2)
    y = jnp.einsum('bsgc,gcd->bsgd', p, pool_w.astype(jnp.float32)).reshape(B, S, POOL_WIDTH)
    return (y * pool_scale.astype(jnp.float32)).astype(u.dtype)


def conv_module(u, dw, db, ln_g, ln_b):
    a, g = jnp.split(u, 2, axis=-1)
    h = a * jax.nn.sigmoid(g)
    h = lax.conv_general_dilated(h, dw[:, None, :].astype(h.dtype), window_strides=(1,),
                                 padding=[(CONV_KERNEL - 1, 0)],
                                 dimension_numbers=('NWC', 'WIO', 'NWC'),
                                 feature_group_count=CONV_WIDTH) + db.astype(h.dtype)
    h = layer_norm(h, ln_g, ln_b)
    return jax.nn.silu(h)


def rotary(t, positions):
    half = t.shape[-1] // 2
    inv = ROPE_BASE ** (-jnp.arange(half, dtype=jnp.float32) / half)
    ang = positions.astype(jnp.float32)[..., None] * inv
    cos = jnp.cos(ang)[:, :, None, :]
    sin = jnp.sin(ang)[:, :, None, :]
    t1, t2 = t[..., :half], t[..., half:]
    return jnp.concatenate([t1 * cos - t2 * sin, t1 * sin + t2 * cos], axis=-1)


def retention(q, k, v, gate, positions, gn_g):
    B, S, _ = q.shape
    H, Dh, C = RET_HEADS, RET_HEAD_DIM, RET_CHUNK
    N = S // C
    qf = rotary(q.astype(jnp.float32).reshape(B, S, H, Dh), positions)
    kf = rotary(k.astype(jnp.float32).reshape(B, S, H, Dh), positions) * (Dh ** -0.5)
    vf = v.astype(jnp.float32).reshape(B, S, H, Dh)

    log_gamma = jnp.log1p(-jnp.exp2(-5.0 - jnp.arange(H, dtype=jnp.float32)))
    idx = jnp.arange(C, dtype=jnp.float32)
    rel = idx[:, None] - idx[None, :]
    decay = jnp.where(rel >= 0, jnp.exp(jnp.maximum(rel, 0.0)[None] * log_gamma[:, None, None]), 0.0)
    xi = jnp.exp((idx + 1.0)[None, :] * log_gamma[:, None])
    zeta = jnp.exp((C - 1.0 - idx)[None, :] * log_gamma[:, None])
    gamma_c = jnp.exp(C * log_gamma)

    def chunks(t):
        return t.reshape(B, N, C, H, Dh).transpose(0, 3, 1, 2, 4)
    qc, kc, vc = chunks(qf), chunks(kf), chunks(vf)

    scores = jnp.einsum('bhncd,bhned->bhnce', qc, kc) * decay[None, :, None]
    intra = jnp.einsum('bhnce,bhned->bhncd', scores, vc)
    kv = jnp.einsum('bhncd,bhnce->bhnde', kc, vc * zeta[None, :, None, :, None])

    def step(state, kv_n):
        return gamma_c[None, :, None, None] * state + kv_n, state
    _, prev = lax.scan(step, jnp.zeros((B, H, Dh, Dh), jnp.float32), kv.transpose(2, 0, 1, 3, 4))
    prev = prev.transpose(1, 2, 0, 3, 4)
    inter = jnp.einsum('bhncd,bhnde->bhnce', qc, prev) * xi[None, :, None, :, None]

    o = (intra + inter).transpose(0, 2, 3, 1, 4).reshape(B, S, H, Dh)
    mu = jnp.mean(o, axis=-1, keepdims=True)
    var = jnp.mean(jnp.square(o - mu), axis=-1, keepdims=True)
    o = ((o - mu) * lax.rsqrt(var + LN_EPS)).reshape(B, S, RET_WIDTH) * gn_g.astype(jnp.float32)
    return (jax.nn.silu(gate.astype(jnp.float32)) * o).astype(q.dtype)


def hybrid_layer(x, positions, g_mix_pre, g_mix_post, g_ffn_pre, g_ffn_post, w_in,
                 pool_w, pool_scale, conv_dw, conv_b, conv_ln_g, conv_ln_b, ret_gn_g,
                 w_pool_proj, w_conv_proj, w_ret_proj, w_out, w_ffn_in, w_ffn_out):
    B, S, D = x.shape
    h = rms_norm(x, g_mix_pre)
    proj = h @ w_in
    u_pool, u_conv, q, k, v, g_ret, gate_pre = jnp.split(proj, IN_OFFSETS, axis=-1)
    y_pool = pool_mixer(u_pool, pool_w, pool_scale) @ w_pool_proj
    y_conv = conv_module(u_conv, conv_dw, conv_b, conv_ln_g, conv_ln_b) @ w_conv_proj
    y_ret = retention(q, k, v, g_ret, positions, ret_gn_g) @ w_ret_proj
    gates = jax.nn.sigmoid(gate_pre.reshape(B, S, N_BRANCH, D))
    merged = gates[:, :, 0] * y_pool + gates[:, :, 1] * y_conv + gates[:, :, 2] * y_ret
    x = x + rms_norm(merged @ w_out, g_mix_post)

    h = rms_norm(x, g_ffn_pre)
    a, b = jnp.split(h @ w_ffn_in, 2, axis=-1)
    x = x + rms_norm((jax.nn.silu(a) * b) @ w_ffn_out, g_ffn_post)
    return x


def setup_inputs(seed: int = 0) -> dict:
    key = jax.random.key(seed)
    ks = jax.random.split(key, 24)

    def nrm(k, shape, scale):
        return jax.random.normal(k, shape, jnp.float32) * scale

    def gain(k, shape):
        return 1.0 + 0.05 * jax.random.normal(k, shape, jnp.float32)

    L, D = DEPTH, D_MODEL
    return {
        "x": nrm(ks[0], (BATCH, SEQ, D), 1.0),
        "positions": jnp.broadcast_to(jnp.arange(SEQ, dtype=jnp.int32), (BATCH, SEQ)),
        "g_mix_pre": gain(ks[1], (L, D)),
        "g_mix_post": gain(ks[2], (L, D)),
        "g_ffn_pre": gain(ks[3], (L, D)),
        "g_ffn_post": gain(ks[4], (L, D)),
        "w_in": nrm(ks[5], (L, D, N_IN), D ** -0.5),
        "pool_w": nrm(ks[6], (L, len(POOL_WINDOWS), POOL_GROUP, POOL_GROUP), POOL_GROUP ** -0.5),
        "pool_scale": gain(ks[7], (L, POOL_WIDTH)),
        "conv_dw": nrm(ks[8], (L, CONV_KERNEL, CONV_WIDTH), CONV_KERNEL ** -0.5),
        "conv_b": nrm(ks[9], (L, CONV_WIDTH), 0.02),
        "conv_ln_g": gain(ks[10], (L, CONV_WIDTH)),
        "conv_ln_b": nrm(ks[11], (L, CONV_WIDTH), 0.02),
        "ret_gn_g": gain(ks[12], (L, RET_WIDTH)),
        "w_pool_proj": nrm(ks[13], (L, POOL_WIDTH, D), POOL_WIDTH ** -0.5),
        "w_conv_proj": nrm(ks[14], (L, CONV_WIDTH, D), CONV_WIDTH ** -0.5),
        "w_ret_proj": nrm(ks[15], (L, RET_WIDTH, D), RET_WIDTH ** -0.5),
        "w_out": nrm(ks[16], (L, D, D), D ** -0.5),
        "w_ffn_in": nrm(ks[17], (L, D, 2 * FFN_HIDDEN), D ** -0.5),
        "w_ffn_out": nrm(ks[18], (L, FFN_HIDDEN, D), FFN_HIDDEN ** -0.5),
    }


def reference(x, positions, g_mix_pre, g_mix_post, g_ffn_pre, g_ffn_post, w_in,
              pool_w, pool_scale, conv_dw, conv_b, conv_ln_g, conv_ln_b, ret_gn_g,
              w_pool_proj, w_conv_proj, w_ret_proj, w_out, w_ffn_in, w_ffn_out):
    for l in range(DEPTH):
        x = hybrid_layer(x, positions, g_mix_pre[l], g_mix_post[l], g_ffn_pre[l], g_ffn_post[l], w_in[l],
                         pool_w[l], pool_scale[l], conv_dw[l], conv_b[l], conv_ln_g[l], conv_ln_b[l],
                         ret_gn_g[l], w_pool_proj[l], w_conv_proj[l], w_ret_proj[l], w_out[l],
                         w_ffn_in[l], w_ffn_out[l])
    return x
```

```python
import functools
import math

import jax
import jax.numpy as jnp
import numpy as np
from jax import lax
from jax.experimental import pallas as pl
from jax.experimental.pallas import tpu as pltpu

F32 = jnp.float32
BF16 = jnp.bfloat16

NORM_EPS = 1e-6
LN_EPS = 1e-5
ROPE_BASE = 10000.0
POOL_WINDOWS = (2, 4, 8, 16)
RET_HEADS = 8
N_BRANCH = 3

LANES = 128
SUBLANES = 8
VMEM_LIMIT_BYTES = 56 * 1024 * 1024

POOL_HALO = 16
CONV_HALO = 32
CONV_ROWS = 32


def _rms_norm(x, g):
    return x * lax.rsqrt(jnp.mean(x * x, axis=-1, keepdims=True) + NORM_EPS) * g


def _compiler_params(semantics):
    return pltpu.CompilerParams(dimension_semantics=semantics, vmem_limit_bytes=VMEM_LIMIT_BYTES)


def _rope_kernel(pos_ref, cos_ref, sin_ref, *, half):
    lane = lax.broadcasted_iota(jnp.int32, (1, 2 * half), 1)
    freq = (lane % half).astype(F32)
    inv = jnp.exp(freq * (-math.log(ROPE_BASE) / half))
    ang = pos_ref[...] * inv
    cos_ref[...] = jnp.cos(ang)
    sin_ref[...] = jnp.where(lane < half, -jnp.sin(ang), jnp.sin(ang))


def _rope_tables(positions, head_dim, tm):
    B, S = positions.shape
    posb = jnp.broadcast_to(positions.astype(F32)[..., None], (B, S, head_dim))
    spec = pl.BlockSpec((None, tm, head_dim), lambda b, s: (b, s, 0))
    return pl.pallas_call(
        functools.partial(_rope_kernel, half=head_dim // 2),
        grid=(B, S // tm),
        in_specs=[spec],
        out_specs=[spec, spec],
        out_shape=[jax.ShapeDtypeStruct((B, S, head_dim), F32)] * 2,
        compiler_params=_compiler_params(("parallel", "parallel")),
        name="rope_tables",
    )(posb)


def _norm_matmul_kernel(x_ref, g_ref, w_ref, o_ref, xn_ref, *, sigmoid):
    @pl.when(pl.program_id(1) == 0)
    def _():
        xn_ref[...] = _rms_norm(x_ref[...], g_ref[...]).astype(BF16)

    acc = jnp.dot(xn_ref[...], w_ref[...], preferred_element_type=F32)
    if sigmoid:
        acc = jax.nn.sigmoid(acc)
    o_ref[...] = acc.astype(o_ref.dtype)


def _norm_matmul(x, g, w, *, tm, tn, sigmoid, name):
    T, D = x.shape
    N = w.shape[1]
    return pl.pallas_call(
        functools.partial(_norm_matmul_kernel, sigmoid=sigmoid),
        grid=(T // tm, N // tn),
        in_specs=[
            pl.BlockSpec((tm, D), lambda i, j: (i, 0)),
            pl.BlockSpec((1, D), lambda i, j: (0, 0)),
            pl.BlockSpec((D, tn), lambda i, j: (0, j)),
        ],
        out_specs=pl.BlockSpec((tm, tn), lambda i, j: (i, j)),
        out_shape=jax.ShapeDtypeStruct((T, N), BF16),
        scratch_shapes=[pltpu.VMEM((tm, D), BF16)],
        compiler_params=_compiler_params(("parallel", "arbitrary")),
        name=name,
    )(x, g, w)


def _mixer_kernel(mix_ref, cos_ref, sin_ref, poolw_ref, pscale_ref, dw_ref, cb_ref, lng_ref, lnb_ref,
                  gng_ref, out_ref, u_scr, h_scr, state_scr, decay_scr, xi_scr, zeta_scr,
                  *, tm, chunk, pool_width, conv_width, conv_kernel, ret_width):
    s = pl.program_id(1)
    heads = RET_HEADS
    dh = ret_width // heads
    pool_group = pool_width // len(POOL_WINDOWS)
    log_gamma = [math.log1p(-(2.0 ** (-5.0 - h))) for h in range(heads)]

    off_a = pool_width
    off_g = off_a + conv_width
    off_q = off_g + conv_width
    off_k = off_q + ret_width
    off_v = off_k + ret_width
    off_gate = off_v + ret_width

    @pl.when((pl.program_id(0) == 0) & (s == 0))
    def _():
        row = lax.broadcasted_iota(jnp.int32, (chunk, chunk), 0).astype(F32)
        col = lax.broadcasted_iota(jnp.int32, (chunk, chunk), 1).astype(F32)
        rel = row - col
        for h in range(heads):
            decay_scr[h] = jnp.where(rel >= 0, jnp.exp(jnp.maximum(rel, 0.0) * log_gamma[h]), 0.0)
            xi_scr[h] = jnp.exp((row + 1.0) * log_gamma[h])
            zeta_scr[h] = jnp.exp((chunk - 1.0 - row) * log_gamma[h])

    @pl.when(s == 0)
    def _():
        u_scr[0:POOL_HALO, :] = jnp.zeros((POOL_HALO, pool_width), F32)
        h_scr[0:CONV_HALO, :] = jnp.zeros((CONV_HALO, conv_width), F32)
        state_scr[...] = jnp.zeros_like(state_scr)

    u = mix_ref[:, 0:pool_width].astype(F32)
    u_scr[POOL_HALO:POOL_HALO + tm, :] = u
    t_idx = s * tm + lax.broadcasted_iota(jnp.int32, (tm, 1), 0)
    for gi, w in enumerate(POOL_WINDOWS):
        c0 = gi * pool_group
        ug = u[:, c0:c0 + pool_group]
        acc = ug
        for k in range(1, w):
            acc = acc + u_scr[POOL_HALO - k:POOL_HALO - k + tm, c0:c0 + pool_group]
        cnt = jnp.minimum(t_idx + 1, w).astype(F32)
        p = acc / cnt - ug
        y = jnp.dot(p.astype(BF16), poolw_ref[gi], preferred_element_type=F32)
        out_ref[:, c0:c0 + pool_group] = (y * pscale_ref[:, c0:c0 + pool_group]).astype(out_ref.dtype)
    u_scr[0:POOL_HALO, :] = u_scr[tm:tm + POOL_HALO, :]

    a = mix_ref[:, off_a:off_a + conv_width].astype(F32)
    g = mix_ref[:, off_g:off_g + conv_width].astype(F32)
    h_scr[CONV_HALO:CONV_HALO + tm, :] = a * jax.nn.sigmoid(g)
    first = CONV_HALO - (conv_kernel - 1)
    for r0 in range(0, tm, CONV_ROWS):
        acc = jnp.broadcast_to(cb_ref[...], (CONV_ROWS, conv_width))
        for j in range(conv_kernel):
            acc = acc + dw_ref[j:j + 1, :] * h_scr[first + r0 + j:first + r0 + j + CONV_ROWS, :]
        mu = jnp.mean(acc, axis=-1, keepdims=True)
        d = acc - mu
        var = jnp.mean(d * d, axis=-1, keepdims=True)
        y = d * lax.rsqrt(var + LN_EPS) * lng_ref[...] + lnb_ref[...]
        out_ref[r0:r0 + CONV_ROWS, pool_width:pool_width + conv_width] = (
            y * jax.nn.sigmoid(y)).astype(out_ref.dtype)
    h_scr[0:CONV_HALO, :] = h_scr[tm:tm + CONV_HALO, :]

    cosf = cos_ref[...]
    sins = sin_ref[...]
    out_off = pool_width + conv_width
    for h in range(heads):
        lanes = slice(h * dh, (h + 1) * dh)
        gamma_c = math.exp(chunk * log_gamma[h])
        for c in range(tm // chunk):
            rows = slice(c * chunk, (c + 1) * chunk)
            qh = mix_ref[rows, off_q + h * dh:off_q + (h + 1) * dh].astype(F32)
            kh = mix_ref[rows, off_k + h * dh:off_k + (h + 1) * dh].astype(F32)
            vh = mix_ref[rows, off_v + h * dh:off_v + (h + 1) * dh].astype(F32)
            qr = (qh * cosf[rows] + pltpu.roll(qh, dh // 2, axis=1) * sins[rows]).astype(BF16)
            kr = ((kh * cosf[rows] + pltpu.roll(kh, dh // 2, axis=1) * sins[rows]) * (dh ** -0.5)).astype(BF16)
            scores = lax.dot_general(qr, kr, (((1,), (1,)), ((), ())), preferred_element_type=F32)
            scores = scores * decay_scr[h]
            intra = jnp.dot(scores.astype(BF16), vh.astype(BF16), preferred_element_type=F32)
            state = state_scr[h]
            inter = jnp.dot(qr, state.astype(BF16), preferred_element_type=F32) * xi_scr[h]
            kv = lax.dot_general(kr, (vh * zeta_scr[h]).astype(BF16), (((0,), (0,)), ((), ())),
                                 preferred_element_type=F32)
            state_scr[h] = gamma_c * state + kv
            o = intra + inter
            mu = jnp.mean(o, axis=-1, keepdims=True)
            d = o - mu
            var = jnp.mean(d * d, axis=-1, keepdims=True)
            o = d * lax.rsqrt(var + LN_EPS) * gng_ref[:, lanes]
            gate = mix_ref[rows, off_gate + h * dh:off_gate + (h + 1) * dh].astype(F32)
            out_ref[rows, out_off + h * dh:out_off + (h + 1) * dh] = (
                gate * jax.nn.sigmoid(gate) * o).astype(out_ref.dtype)


def _mixers(mix, cosf, sins, pool_w, pool_scale, conv_dw, conv_b, ln_g, ln_b, gn_g, *, tm, chunk, d_model):
    B, S, n_mix = mix.shape
    pool_width = pool_scale.shape[-1]
    conv_kernel, conv_width = conv_dw.shape
    ret_width = gn_g.shape[-1]
    dh = ret_width // RET_HEADS
    assert n_mix == pool_width + 2 * conv_width + 4 * ret_width
    assert pool_width + conv_width + ret_width == d_model
    assert dh == LANES and chunk % SUBLANES == 0 and tm % chunk == 0 and tm % CONV_ROWS == 0
    assert max(POOL_WINDOWS) - 1 <= POOL_HALO and conv_kernel - 1 <= CONV_HALO
    n_groups = len(POOL_WINDOWS)
    pool_group = pool_width // n_groups

    def row(v):
        return v.reshape(1, -1)

    def const_spec(shape):
        return pl.BlockSpec(shape, lambda b, s: (0,) * len(shape))

    kern = functools.partial(_mixer_kernel, tm=tm, chunk=chunk, pool_width=pool_width,
                             conv_width=conv_width, conv_kernel=conv_kernel, ret_width=ret_width)
    return pl.pallas_call(
        kern,
        grid=(B, S // tm),
        in_specs=[
            pl.BlockSpec((None, tm, n_mix), lambda b, s: (b, s, 0)),
            pl.BlockSpec((None, tm, dh), lambda b, s: (b, s, 0)),
            pl.BlockSpec((None, tm, dh), lambda b, s: (b, s, 0)),
            const_spec((n_groups, pool_group, pool_group)),
            const_spec((1, pool_width)),
            const_spec((conv_kernel, conv_width)),
            const_spec((1, conv_width)),
            const_spec((1, conv_width)),
            const_spec((1, conv_width)),
            const_spec((1, ret_width)),
        ],
        out_specs=pl.BlockSpec((None, tm, d_model), lambda b, s: (b, s, 0)),
        out_shape=jax.ShapeDtypeStruct((B, S, d_model), BF16),
        scratch_shapes=[
            pltpu.VMEM((POOL_HALO + tm, pool_width), F32),
            pltpu.VMEM((CONV_HALO + tm, conv_width), F32),
            pltpu.VMEM((RET_HEADS, dh, dh), F32),
            pltpu.VMEM((RET_HEADS, chunk, chunk), F32),
            pltpu.VMEM((RET_HEADS, chunk, dh), F32),
            pltpu.VMEM((RET_HEADS, chunk, dh), F32),
        ],
        compiler_params=_compiler_params(("arbitrary", "arbitrary")),
        name="mixers",
    )(mix, cosf, sins, pool_w, row(pool_scale), conv_dw, row(conv_b), row(ln_g), row(ln_b), row(gn_g))


def _merge_kernel(br_ref, gate_ref, x_ref, wp_ref, wc_ref, wr_ref, wo_ref, g_ref, o_ref,
                  *, pool_width, conv_width):
    d = x_ref.shape[-1]
    c1 = pool_width
    c2 = pool_width + conv_width
    y_pool = jnp.dot(br_ref[:, 0:c1], wp_ref[...], preferred_element_type=F32)
    merged = gate_ref[:, 0:d].astype(F32) * y_pool
    y_conv = jnp.dot(br_ref[:, c1:c2], wc_ref[...], preferred_element_type=F32)
    merged = merged + gate_ref[:, d:2 * d].astype(F32) * y_conv
    y_ret = jnp.dot(br_ref[:, c2:d], wr_ref[...], preferred_element_type=F32)
    merged = merged + gate_ref[:, 2 * d:3 * d].astype(F32) * y_ret
    out = jnp.dot(merged.astype(BF16), wo_ref[...], preferred_element_type=F32)
    o_ref[...] = x_ref[...] + _rms_norm(out, g_ref[...])


def _merge(br, gates, x, wp, wc, wr, wo, g, *, tm):
    T, D = x.shape
    pool_width, conv_width, ret_width = wp.shape[0], wc.shape[0], wr.shape[0]

    def resident(shape):
        return pl.BlockSpec(shape, lambda i: (0, 0), pipeline_mode=pl.Buffered(1))

    return pl.pallas_call(
        functools.partial(_merge_kernel, pool_width=pool_width, conv_width=conv_width),
        grid=(T // tm,),
        in_specs=[
            pl.BlockSpec((tm, D), lambda i: (i, 0)),
            pl.BlockSpec((tm, N_BRANCH * D), lambda i: (i, 0)),
            pl.BlockSpec((tm, D), lambda i: (i, 0)),
            resident((pool_width, D)),
            resident((conv_width, D)),
            resident((ret_width, D)),
            resident((D, D)),
            resident((1, D)),
        ],
        out_specs=pl.BlockSpec((tm, D), lambda i: (i, 0)),
        out_shape=jax.ShapeDtypeStruct((T, D), F32),
        compiler_params=_compiler_params(("parallel",)),
        name="merge",
    )(br, gates, x, wp, wc, wr, wo, g)


def _ffn_kernel(x_ref, gpre_ref, wa_ref, wb_ref, wo_ref, gpost_ref, o_ref, xn_ref, acc_ref):
    j = pl.program_id(1)

    @pl.when(j == 0)
    def _():
        xn_ref[...] = _rms_norm(x_ref[...], gpre_ref[...]).astype(BF16)
        acc_ref[...] = jnp.zeros_like(acc_ref)

    xn = xn_ref[...]
    a = jnp.dot(xn, wa_ref[...], preferred_element_type=F32)
    b = jnp.dot(xn, wb_ref[...], preferred_element_type=F32)
    hid = (a * jax.nn.sigmoid(a) * b).astype(BF16)
    acc_ref[...] += jnp.dot(hid, wo_ref[...], preferred_element_type=F32)

    @pl.when(j == pl.num_programs(1) - 1)
    def _():
        o_ref[...] = x_ref[...] + _rms_norm(acc_ref[...], gpost_ref[...])


def _ffn(x, g_pre, w_in, w_out, g_post, *, tm, th):
    T, D = x.shape
    hidden = w_out.shape[0]
    nh = hidden // th
    return pl.pallas_call(
        _ffn_kernel,
        grid=(T // tm, nh),
        in_specs=[
            pl.BlockSpec((tm, D), lambda i, j: (i, 0)),
            pl.BlockSpec((1, D), lambda i, j: (0, 0)),
            pl.BlockSpec((D, th), lambda i, j: (0, j)),
            pl.BlockSpec((D, th), lambda i, j: (0, j + nh)),
            pl.BlockSpec((th, D), lambda i, j: (j, 0)),
            pl.BlockSpec((1, D), lambda i, j: (0, 0)),
        ],
        out_specs=pl.BlockSpec((tm, D), lambda i, j: (i, 0)),
        out_shape=jax.ShapeDtypeStruct((T, D), F32),
        scratch_shapes=[pltpu.VMEM((tm, D), BF16), pltpu.VMEM((tm, D), F32)],
        compiler_params=_compiler_params(("parallel", "arbitrary")),
        name="ffn",
    )(x, g_pre, w_in, w_in, w_out, g_post)


def _pick(n, candidates):
    for c in candidates:
        if n % c == 0:
            return c
    raise ValueError(f"no tile in {candidates} divides {n}")


def kernel(x, positions, g_mix_pre, g_mix_post, g_ffn_pre, g_ffn_post, w_in, pool_w, pool_scale, conv_dw,
           conv_b, conv_ln_g, conv_ln_b, ret_gn_g, w_pool_proj, w_conv_proj, w_ret_proj, w_out, w_ffn_in,
           w_ffn_out):
    B, S, D = x.shape
    depth = w_in.shape[0]
    T = B * S
    pool_width = pool_scale.shape[-1]
    conv_width = conv_b.shape[-1]
    ret_width = ret_gn_g.shape[-1]
    n_mix = pool_width + 2 * conv_width + 4 * ret_width
    hidden = w_ffn_out.shape[1]
    head_dim = ret_width // RET_HEADS
    chunk = 128

    tm_proj = _pick(T, (1024, 512, 256, 128))
    tn_proj = _pick(math.gcd(n_mix, N_BRANCH * D), (512, 256, 128))
    tm_mix = _pick(S, (256, 128))
    tm_merge = _pick(T, (256, 128))
    tm_ffn = _pick(T, (512, 256, 128))
    th_ffn = _pick(hidden, (512, 256, 128))

    w_mix = w_in[:, :, :n_mix].astype(BF16)
    w_gate = w_in[:, :, n_mix:].astype(BF16)
    pool_w16 = pool_w.astype(BF16)
    wp16 = w_pool_proj.astype(BF16)
    wc16 = w_conv_proj.astype(BF16)
    wr16 = w_ret_proj.astype(BF16)
    wo16 = w_out.astype(BF16)
    wfi16 = w_ffn_in.astype(BF16)
    wfo16 = w_ffn_out.astype(BF16)

    cosf, sins = _rope_tables(positions, head_dim, tm_mix)

    xf = x.reshape(T, D)
    for l in range(depth):
        g_pre = g_mix_pre[l].reshape(1, D)
        mix = _norm_matmul(xf, g_pre, w_mix[l], tm=tm_proj, tn=tn_proj, sigmoid=False, name="proj_mix")
        gates = _norm_matmul(xf, g_pre, w_gate[l], tm=tm_proj, tn=tn_proj, sigmoid=True, name="proj_gate")
        br = _mixers(mix.reshape(B, S, n_mix), cosf, sins, pool_w16[l], pool_scale[l], conv_dw[l], conv_b[l],
                     conv_ln_g[l], conv_ln_b[l], ret_gn_g[l], tm=tm_mix, chunk=chunk, d_model=D)
        xf = _merge(br.reshape(T, D), gates, xf, wp16[l], wc16[l], wr16[l], wo16[l],
                    g_mix_post[l].reshape(1, D), tm=tm_merge)
        xf = _ffn(xf, g_ffn_pre[l].reshape(1, D), wfi16[l], wfo16[l], g_ffn_post[l].reshape(1, D),
                  tm=tm_ffn, th=th_ffn)
    return xf.reshape(B, S, D)
```

```python
import functools
import math

import jax
import jax.numpy as jnp
from jax import lax
from jax.experimental import pallas as pl
from jax.experimental.pallas import tpu as pltpu

F32 = jnp.float32
BF16 = jnp.bfloat16

NORM_EPS = 1e-6
LN_EPS = 1e-5
ROPE_BASE = 10000.0
POOL_WINDOWS = (2, 4, 8, 16)
RET_HEADS = 8
N_BRANCH = 3

LANES = 128
SUBLANES = 8
VMEM_LIMIT_BYTES = 56 * 1024 * 1024

POOL_HALO = 16
CONV_HALO = 32
CONV_ROWS = 64


def _rms_norm(x, g):
    return x * lax.rsqrt(jnp.mean(x * x, axis=-1, keepdims=True) + NORM_EPS) * g


def _compiler_params(semantics):
    return pltpu.CompilerParams(dimension_semantics=semantics, vmem_limit_bytes=VMEM_LIMIT_BYTES)


def _pick(n, candidates):
    for c in candidates:
        if n % c == 0:
            return c
    raise ValueError(f"no tile in {candidates} divides {n}")


def _rope_kernel(pos_ref, cos_ref, sin_ref, *, half):
    lane = lax.broadcasted_iota(jnp.int32, (1, 2 * half), 1)
    freq = (lane % half).astype(F32)
    inv = jnp.exp(freq * (-math.log(ROPE_BASE) / half))
    ang = pos_ref[...] * inv
    cos_ref[...] = jnp.cos(ang)
    sin_ref[...] = jnp.where(lane < half, -jnp.sin(ang), jnp.sin(ang))


def _rope_tables(positions, head_dim):
    B, S = positions.shape
    T = B * S
    tm = _pick(T, (512, 256, 128))
    posb = jnp.broadcast_to(positions.astype(F32).reshape(T, 1), (T, head_dim))
    spec = pl.BlockSpec((tm, head_dim), lambda i: (i, 0))
    return pl.pallas_call(
        functools.partial(_rope_kernel, half=head_dim // 2),
        grid=(T // tm,),
        in_specs=[spec],
        out_specs=[spec, spec],
        out_shape=[jax.ShapeDtypeStruct((T, head_dim), F32)] * 2,
        compiler_params=_compiler_params(("parallel",)),
        name="rope_tables",
    )(posb)


def _norm_kernel(x_ref, g_ref, o_ref):
    o_ref[...] = _rms_norm(x_ref[...], g_ref[...]).astype(o_ref.dtype)


def _norm(x, g):
    T, D = x.shape
    tm = _pick(T, (512, 256, 128))
    return pl.pallas_call(
        _norm_kernel,
        grid=(T // tm,),
        in_specs=[pl.BlockSpec((tm, D), lambda i: (i, 0)), pl.BlockSpec((1, D), lambda i: (0, 0))],
        out_specs=pl.BlockSpec((tm, D), lambda i: (i, 0)),
        out_shape=jax.ShapeDtypeStruct((T, D), BF16),
        compiler_params=_compiler_params(("parallel",)),
        name="norm_in",
    )(x, g)


def _rotary(t, cos, sin, head_dim):
    outs = []
    for b in range(t.shape[1] // head_dim):
        tb = t[:, b * head_dim:(b + 1) * head_dim]
        outs.append(tb * cos + pltpu.roll(tb, head_dim // 2, axis=1) * sin)
    return outs[0] if len(outs) == 1 else jnp.concatenate(outs, axis=1)


def _proj_mix_kernel(x_ref, w_ref, cos_ref, sin_ref, o_ref, *, q_tiles, k_tiles, head_dim):
    j = pl.program_id(1)
    is_q = (j >= q_tiles[0]) & (j < q_tiles[1])
    is_k = (j >= k_tiles[0]) & (j < k_tiles[1])

    @pl.when(jnp.logical_not(is_q | is_k))
    def _():
        o_ref[...] = jnp.dot(x_ref[...], w_ref[...], preferred_element_type=F32).astype(o_ref.dtype)

    @pl.when(is_q)
    def _():
        acc = jnp.dot(x_ref[...], w_ref[...], preferred_element_type=F32)
        o_ref[...] = _rotary(acc, cos_ref[...], sin_ref[...], head_dim).astype(o_ref.dtype)

    @pl.when(is_k)
    def _():
        acc = jnp.dot(x_ref[...], w_ref[...], preferred_element_type=F32)
        o_ref[...] = (_rotary(acc, cos_ref[...], sin_ref[...], head_dim) * (head_dim ** -0.5)).astype(o_ref.dtype)


def _proj_mix(xn, w, layer, cosf, sins, *, tm, tn, q_cols, k_cols, head_dim):
    T, D = xn.shape
    N = w.shape[2]
    assert all(c % tn == 0 for c in q_cols + k_cols) and tn % head_dim == 0
    kern = functools.partial(_proj_mix_kernel, q_tiles=tuple(c // tn for c in q_cols),
                             k_tiles=tuple(c // tn for c in k_cols), head_dim=head_dim)
    return pl.pallas_call(
        kern,
        grid=(T // tm, N // tn),
        in_specs=[
            pl.BlockSpec((tm, D), lambda i, j: (i, 0)),
            pl.BlockSpec((None, D, tn), lambda i, j: (layer, 0, j)),
            pl.BlockSpec((tm, head_dim), lambda i, j: (i, 0)),
            pl.BlockSpec((tm, head_dim), lambda i, j: (i, 0)),
        ],
        out_specs=pl.BlockSpec((tm, tn), lambda i, j: (i, j)),
        out_shape=jax.ShapeDtypeStruct((T, N), BF16),
        compiler_params=_compiler_params(("parallel", "arbitrary")),
        name="proj_mix",
    )(xn, w, cosf, sins)


def _proj_gate_kernel(x_ref, w_ref, o_ref):
    acc = jnp.dot(x_ref[...], w_ref[...], preferred_element_type=F32)
    o_ref[...] = jax.nn.sigmoid(acc).astype(o_ref.dtype)


def _proj_gate(xn, w, layer, *, tm, tn):
    T, D = xn.shape
    N = w.shape[2]
    return pl.pallas_call(
        _proj_gate_kernel,
        grid=(T // tm, N // tn),
        in_specs=[
            pl.BlockSpec((tm, D), lambda i, j: (i, 0)),
            pl.BlockSpec((None, D, tn), lambda i, j: (layer, 0, j)),
        ],
        out_specs=pl.BlockSpec((tm, tn), lambda i, j: (i, j)),
        out_shape=jax.ShapeDtypeStruct((T, N), BF16),
        compiler_params=_compiler_params(("parallel", "arbitrary")),
        name="proj_gate",
    )(xn, w)


def _mixer_kernel(mix_ref, poolw_ref, pscale_ref, dw_ref, cb_ref, lng_ref, lnb_ref, gng_ref, out_ref,
                  u_scr, h_scr, state_scr, decay_scr, xi_scr, zeta_scr,
                  *, tm, chunk, pool_width, conv_width, conv_kernel, ret_width):
    s = pl.program_id(1)
    heads = RET_HEADS
    dh = ret_width // heads
    pool_group = pool_width // len(POOL_WINDOWS)
    log_gamma = [math.log1p(-(2.0 ** (-5.0 - h))) for h in range(heads)]

    off_a = pool_width
    off_g = off_a + conv_width
    off_q = off_g + conv_width
    off_k = off_q + ret_width
    off_v = off_k + ret_width
    off_gate = off_v + ret_width

    @pl.when((pl.program_id(0) == 0) & (s == 0))
    def _():
        row = lax.broadcasted_iota(jnp.int32, (chunk, chunk), 0).astype(F32)
        col = lax.broadcasted_iota(jnp.int32, (chunk, chunk), 1).astype(F32)
        rel = row - col
        for h in range(heads):
            decay_scr[h] = jnp.where(rel >= 0, jnp.exp(jnp.maximum(rel, 0.0) * log_gamma[h]), 0.0)
            xi_scr[h] = jnp.exp((row + 1.0) * log_gamma[h])
            zeta_scr[h] = jnp.exp((chunk - 1.0 - row) * log_gamma[h])

    @pl.when(s == 0)
    def _():
        u_scr[0:POOL_HALO, :] = jnp.zeros((POOL_HALO, pool_width), F32)
        h_scr[0:CONV_HALO, :] = jnp.zeros((CONV_HALO, conv_width), F32)
        state_scr[...] = jnp.zeros_like(state_scr)

    u_scr[POOL_HALO:POOL_HALO + tm, :] = mix_ref[:, 0:pool_width].astype(F32)
    t_idx = s * tm + lax.broadcasted_iota(jnp.int32, (tm, 1), 0)
    for gi, w in enumerate(POOL_WINDOWS):
        c0 = gi * pool_group
        ug = u_scr[:, c0:c0 + pool_group]
        win = ug
        span = 1
        while span < w:
            win = win + pltpu.roll(win, span, axis=0)
            span *= 2
        cnt = jnp.minimum(t_idx + 1, w).astype(F32)
        p = win[POOL_HALO:, :] / cnt - ug[POOL_HALO:, :]
        y = jnp.dot(p.astype(BF16), poolw_ref[gi], preferred_element_type=F32)
        out_ref[:, c0:c0 + pool_group] = (y * pscale_ref[:, c0:c0 + pool_group]).astype(out_ref.dtype)
    u_scr[0:POOL_HALO, :] = u_scr[tm:tm + POOL_HALO, :]

    a_in = mix_ref[:, off_a:off_a + conv_width].astype(F32)
    g_in = mix_ref[:, off_g:off_g + conv_width].astype(F32)
    h_scr[CONV_HALO:CONV_HALO + tm, :] = a_in * jax.nn.sigmoid(g_in)
    win_rows = CONV_ROWS + SUBLANES
    for r0 in range(0, tm, CONV_ROWS):
        base = CONV_HALO + r0 - SUBLANES
        blocks = []
        for c0 in range(0, conv_width, LANES):
            cols = slice(c0, c0 + LANES)
            acc = None
            for r in range(SUBLANES):
                part = None
                for a in range((conv_kernel - 1 - r) // SUBLANES + 1):
                    j = conv_kernel - 1 - (SUBLANES * a + r)
                    term = dw_ref[j:j + 1, cols] * h_scr[base - SUBLANES * a:base - SUBLANES * a + win_rows, cols]
                    part = term if part is None else part + term
                if r:
                    part = pltpu.roll(part, r, axis=0)
                acc = part if acc is None else acc + part
            blocks.append(acc[SUBLANES:, :] + cb_ref[:, cols])
        conv = jnp.concatenate(blocks, axis=1)
        mu = jnp.mean(conv, axis=-1, keepdims=True)
        d = conv - mu
        var = jnp.mean(d * d, axis=-1, keepdims=True)
        y = d * lax.rsqrt(var + LN_EPS) * lng_ref[...] + lnb_ref[...]
        out_ref[r0:r0 + CONV_ROWS, pool_width:pool_width + conv_width] = (
            y * jax.nn.sigmoid(y)).astype(out_ref.dtype)
    h_scr[0:CONV_HALO, :] = h_scr[tm:tm + CONV_HALO, :]

    out_off = pool_width + conv_width
    for h in range(heads):
        lanes = slice(h * dh, (h + 1) * dh)
        gamma_c = math.exp(chunk * log_gamma[h])
        for c in range(tm // chunk):
            rows = slice(c * chunk, (c + 1) * chunk)
            q = mix_ref[rows, off_q + h * dh:off_q + (h + 1) * dh]
            k = mix_ref[rows, off_k + h * dh:off_k + (h + 1) * dh]
            v = mix_ref[rows, off_v + h * dh:off_v + (h + 1) * dh]
            scores = lax.dot_general(q, k, (((1,), (1,)), ((), ())), preferred_element_type=F32)
            scores = scores * decay_scr[h]
            intra = jnp.dot(scores.astype(BF16), v, preferred_element_type=F32)
            state = state_scr[h]
            inter = jnp.dot(q, state.astype(BF16), preferred_element_type=F32) * xi_scr[h]
            vz = (v.astype(F32) * zeta_scr[h]).astype(BF16)
            kv = lax.dot_general(k, vz, (((0,), (0,)), ((), ())), preferred_element_type=F32)
            state_scr[h] = gamma_c * state + kv
            o = intra + inter
            mu = jnp.mean(o, axis=-1, keepdims=True)
            d = o - mu
            var = jnp.mean(d * d, axis=-1, keepdims=True)
            o = d * lax.rsqrt(var + LN_EPS) * gng_ref[:, lanes]
            gate = mix_ref[rows, off_gate + h * dh:off_gate + (h + 1) * dh].astype(F32)
            out_ref[rows, out_off + h * dh:out_off + (h + 1) * dh] = (
                gate * jax.nn.sigmoid(gate) * o).astype(out_ref.dtype)


def _mixers(mix, pool_w, layer, pool_scale, conv_dw, conv_b, ln_g, ln_b, gn_g, *, tm, chunk, d_model):
    B, S, n_mix = mix.shape
    pool_width = pool_scale.shape[-1]
    conv_kernel, conv_width = conv_dw.shape
    ret_width = gn_g.shape[-1]
    dh = ret_width // RET_HEADS
    n_groups = len(POOL_WINDOWS)
    pool_group = pool_width // n_groups
    assert n_mix == pool_width + 2 * conv_width + 4 * ret_width
    assert pool_width + conv_width + ret_width == d_model
    assert dh == LANES and pool_group == LANES and conv_width % LANES == 0
    assert chunk % SUBLANES == 0 and tm % chunk == 0 and tm % CONV_ROWS == 0
    assert max(POOL_WINDOWS) - 1 <= POOL_HALO and SUBLANES * ((conv_kernel - 1) // SUBLANES + 1) <= CONV_HALO

    def row(v):
        return v.reshape(1, -1)

    def const_spec(shape):
        return pl.BlockSpec(shape, lambda b, s: (0,) * len(shape))

    kern = functools.partial(_mixer_kernel, tm=tm, chunk=chunk, pool_width=pool_width,
                             conv_width=conv_width, conv_kernel=conv_kernel, ret_width=ret_width)
    return pl.pallas_call(
        kern,
        grid=(B, S // tm),
        in_specs=[
            pl.BlockSpec((None, tm, n_mix), lambda b, s: (b, s, 0)),
            pl.BlockSpec((None, n_groups, pool_group, pool_group), lambda b, s: (layer, 0, 0, 0)),
            const_spec((1, pool_width)),
            const_spec((conv_kernel, conv_width)),
            const_spec((1, conv_width)),
            const_spec((1, conv_width)),
            const_spec((1, conv_width)),
            const_spec((1, ret_width)),
        ],
        out_specs=pl.BlockSpec((None, tm, d_model), lambda b, s: (b, s, 0)),
        out_shape=jax.ShapeDtypeStruct((B, S, d_model), BF16),
        scratch_shapes=[
            pltpu.VMEM((POOL_HALO + tm, pool_width), F32),
            pltpu.VMEM((CONV_HALO + tm, conv_width), F32),
            pltpu.VMEM((RET_HEADS, dh, dh), F32),
            pltpu.VMEM((RET_HEADS, chunk, chunk), F32),
            pltpu.VMEM((RET_HEADS, chunk, dh), F32),
            pltpu.VMEM((RET_HEADS, chunk, dh), F32),
        ],
        compiler_params=_compiler_params(("arbitrary", "arbitrary")),
        name="mixers",
    )(mix, pool_w, row(pool_scale), conv_dw, row(conv_b), row(ln_g), row(ln_b), row(gn_g))


def _merge_kernel(br_ref, gate_ref, x_ref, wp_ref, wc_ref, wr_ref, wo_ref, g_ref, gn_ref, o_ref, xn_ref,
                  *, pool_width, conv_width):
    d = x_ref.shape[-1]
    c1 = pool_width
    c2 = pool_width + conv_width
    y_pool = jnp.dot(br_ref[:, 0:c1], wp_ref[...], preferred_element_type=F32)
    merged = gate_ref[:, 0:d].astype(F32) * y_pool
    y_conv = jnp.dot(br_ref[:, c1:c2], wc_ref[...], preferred_element_type=F32)
    merged = merged + gate_ref[:, d:2 * d].astype(F32) * y_conv
    y_ret = jnp.dot(br_ref[:, c2:d], wr_ref[...], preferred_element_type=F32)
    merged = merged + gate_ref[:, 2 * d:3 * d].astype(F32) * y_ret
    out = jnp.dot(merged.astype(BF16), wo_ref[...], preferred_element_type=F32)
    x_new = x_ref[...] + _rms_norm(out, g_ref[...])
    o_ref[...] = x_new
    xn_ref[...] = _rms_norm(x_new, gn_ref[...]).astype(xn_ref.dtype)


def _merge(br, gates, x, wp, wc, wr, wo, layer, g_post, g_next, *, tm):
    T, D = x.shape
    pool_width, conv_width, ret_width = wp.shape[1], wc.shape[1], wr.shape[1]

    def resident(rows):
        return pl.BlockSpec((None, rows, D), lambda i: (layer, 0, 0), pipeline_mode=pl.Buffered(1))

    def vec():
        return pl.BlockSpec((1, D), lambda i: (0, 0))

    def rows(width):
        return pl.BlockSpec((tm, width), lambda i: (i, 0))

    return pl.pallas_call(
        functools.partial(_merge_kernel, pool_width=pool_width, conv_width=conv_width),
        grid=(T // tm,),
        in_specs=[rows(D), rows(N_BRANCH * D), rows(D), resident(pool_width), resident(conv_width),
                  resident(ret_width), resident(D), vec(), vec()],
        out_specs=[rows(D), rows(D)],
        out_shape=[jax.ShapeDtypeStruct((T, D), F32), jax.ShapeDtypeStruct((T, D), BF16)],
        compiler_params=_compiler_params(("parallel",)),
        name="merge",
    )(br, gates, x, wp, wc, wr, wo, g_post, g_next)


def _ffn_kernel(xn_ref, x_ref, wa_ref, wb_ref, wo_ref, gpost_ref, *rest, emit_next):
    if emit_next:
        gnext_ref, o_ref, xn_out_ref = rest
    else:
        (o_ref,) = rest
    j = pl.program_id(1)

    @pl.when(j == 0)
    def _():
        o_ref[...] = jnp.zeros_like(o_ref)

    xn = xn_ref[...]
    a = jnp.dot(xn, wa_ref[...], preferred_element_type=F32)
    b = jnp.dot(xn, wb_ref[...], preferred_element_type=F32)
    hid = (a * jax.nn.sigmoid(a) * b).astype(BF16)
    o_ref[...] += jnp.dot(hid, wo_ref[...], preferred_element_type=F32)

    @pl.when(j == pl.num_programs(1) - 1)
    def _():
        x_new = x_ref[...] + _rms_norm(o_ref[...], gpost_ref[...])
        o_ref[...] = x_new
        if emit_next:
            xn_out_ref[...] = _rms_norm(x_new, gnext_ref[...]).astype(xn_out_ref.dtype)


def _ffn(xn, x, w_in, w_out, layer, g_post, g_next, *, tm, th):
    T, D = x.shape
    hidden = w_out.shape[1]
    nh = hidden // th
    emit_next = g_next is not None

    def rows():
        return pl.BlockSpec((tm, D), lambda i, j: (i, 0))

    def vec():
        return pl.BlockSpec((1, D), lambda i, j: (0, 0))

    in_specs = [
        rows(), rows(),
        pl.BlockSpec((None, D, th), lambda i, j: (layer, 0, j)),
        pl.BlockSpec((None, D, th), lambda i, j: (layer, 0, j + nh)),
        pl.BlockSpec((None, th, D), lambda i, j: (layer, j, 0)),
        vec(),
    ]
    args = [xn, x, w_in, w_in, w_out, g_post]
    out_specs = [rows()]
    out_shape = [jax.ShapeDtypeStruct((T, D), F32)]
    if emit_next:
        in_specs.append(vec())
        args.append(g_next)
        out_specs.append(rows())
        out_shape.append(jax.ShapeDtypeStruct((T, D), BF16))
    outs = pl.pallas_call(
        functools.partial(_ffn_kernel, emit_next=emit_next),
        grid=(T // tm, nh),
        in_specs=in_specs,
        out_specs=out_specs,
        out_shape=out_shape,
        compiler_params=_compiler_params(("parallel", "arbitrary")),
        name="ffn",
    )(*args)
    return (outs[0], outs[1]) if emit_next else (outs[0], None)


def kernel(x, positions, g_mix_pre, g_mix_post, g_ffn_pre, g_ffn_post, w_in, pool_w, pool_scale, conv_dw,
           conv_b, conv_ln_g, conv_ln_b, ret_gn_g, w_pool_proj, w_conv_proj, w_ret_proj, w_out, w_ffn_in,
           w_ffn_out):
    B, S, D = x.shape
    depth = w_in.shape[0]
    T = B * S
    pool_width = pool_scale.shape[-1]
    conv_width = conv_b.shape[-1]
    ret_width = ret_gn_g.shape[-1]
    n_mix = pool_width + 2 * conv_width + 4 * ret_width
    hidden = w_ffn_out.shape[1]
    head_dim = ret_width // RET_HEADS
    chunk = 128
    q_off = pool_width + 2 * conv_width
    q_cols = (q_off, q_off + ret_width)
    k_cols = (q_off + ret_width, q_off + 2 * ret_width)

    tm_proj = _pick(T, (2048, 1024, 512, 256, 128))
    tn_mix = _pick(math.gcd(math.gcd(pool_width, conv_width), ret_width), (512, 256, 128))
    tn_gate = _pick(N_BRANCH * D, (1024, 512, 256, 128))
    tm_mix = _pick(S, (256, 128))
    tm_merge = _pick(T, (256, 128))
    tm_ffn = _pick(T, (512, 256, 128))
    th_ffn = _pick(hidden, (512, 256, 128))

    w_mix = w_in[:, :, :n_mix].astype(BF16)
    w_gate = w_in[:, :, n_mix:].astype(BF16)
    pool_w16 = pool_w.astype(BF16)
    wp16 = w_pool_proj.astype(BF16)
    wc16 = w_conv_proj.astype(BF16)
    wr16 = w_ret_proj.astype(BF16)
    wo16 = w_out.astype(BF16)
    wfi16 = w_ffn_in.astype(BF16)
    wfo16 = w_ffn_out.astype(BF16)

    cosf, sins = _rope_tables(positions, head_dim)

    xf = x.reshape(T, D)
    xn = _norm(xf, g_mix_pre[0].reshape(1, D))
    for l in range(depth):
        mix = _proj_mix(xn, w_mix, l, cosf, sins, tm=tm_proj, tn=tn_mix, q_cols=q_cols, k_cols=k_cols,
                        head_dim=head_dim)
        gates = _proj_gate(xn, w_gate, l, tm=tm_proj, tn=tn_gate)
        br = _mixers(mix.reshape(B, S, n_mix), pool_w16, l, pool_scale[l], conv_dw[l], conv_b[l],
                     conv_ln_g[l], conv_ln_b[l], ret_gn_g[l], tm=tm_mix, chunk=chunk, d_model=D)
        xf, xn = _merge(br.reshape(T, D), gates, xf, wp16, wc16, wr16, wo16, l,
                        g_mix_post[l].reshape(1, D), g_ffn_pre[l].reshape(1, D), tm=tm_merge)
        g_next = g_mix_pre[l + 1].reshape(1, D) if l + 1 < depth else None
        xf, xn = _ffn(xn, xf, wfi16, wfo16, l, g_ffn_post[l].reshape(1, D), g_next, tm=tm_ffn, th=th_ffn)
    return xf.reshape(B, S, D)
```

```python
import functools
import math

import jax
import jax.numpy as jnp
from jax import lax
from jax.experimental import pallas as pl
from jax.experimental.pallas import tpu as pltpu

F32 = jnp.float32
BF16 = jnp.bfloat16

NORM_EPS = 1e-6
LN_EPS = 1e-5
ROPE_BASE = 10000.0
POOL_WINDOWS = (2, 4, 8, 16)
RET_HEADS = 8
N_BRANCH = 3

LANES = 128
SUBLANES = 8
VMEM_LIMIT_BYTES = 56 * 1024 * 1024

POOL_HALO = 16
CONV_HALO = 32
CONV_ROWS = 64
RET_HEAD_GROUP = 4


def _rms_norm(x, g):
    return x * lax.rsqrt(jnp.mean(x * x, axis=-1, keepdims=True) + NORM_EPS) * g


def _compiler_params(semantics):
    return pltpu.CompilerParams(dimension_semantics=semantics, vmem_limit_bytes=VMEM_LIMIT_BYTES)


def _pick(n, candidates):
    for c in candidates:
        if n % c == 0:
            return c
    raise ValueError(f"no tile in {candidates} divides {n}")


def _rope_kernel(pos_ref, cos_ref, sin_ref, *, half):
    lane = lax.broadcasted_iota(jnp.int32, (1, 2 * half), 1)
    freq = (lane % half).astype(F32)
    inv = jnp.exp(freq * (-math.log(ROPE_BASE) / half))
    ang = pos_ref[...] * inv
    cos_ref[...] = jnp.cos(ang)
    sin_ref[...] = jnp.where(lane < half, -jnp.sin(ang), jnp.sin(ang))


def _rope_tables(positions, head_dim):
    B, S = positions.shape
    T = B * S
    tm = _pick(T, (512, 256, 128))
    posb = jnp.broadcast_to(positions.astype(F32).reshape(T, 1), (T, head_dim))
    spec = pl.BlockSpec((tm, head_dim), lambda i: (i, 0))
    return pl.pallas_call(
        functools.partial(_rope_kernel, half=head_dim // 2),
        grid=(T // tm,),
        in_specs=[spec],
        out_specs=[spec, spec],
        out_shape=[jax.ShapeDtypeStruct((T, head_dim), F32)] * 2,
        compiler_params=_compiler_params(("parallel",)),
        name="rope_tables",
    )(posb)


def _norm_kernel(x_ref, g_ref, o_ref):
    o_ref[...] = _rms_norm(x_ref[...], g_ref[...]).astype(o_ref.dtype)


def _norm(x, g):
    T, D = x.shape
    tm = _pick(T, (512, 256, 128))
    return pl.pallas_call(
        _norm_kernel,
        grid=(T // tm,),
        in_specs=[pl.BlockSpec((tm, D), lambda i: (i, 0)), pl.BlockSpec((1, D), lambda i: (0, 0))],
        out_specs=pl.BlockSpec((tm, D), lambda i: (i, 0)),
        out_shape=jax.ShapeDtypeStruct((T, D), BF16),
        compiler_params=_compiler_params(("parallel",)),
        name="norm_in",
    )(x, g)


def _proj_gate_kernel(x_ref, w_ref, o_ref):
    acc = jnp.dot(x_ref[...], w_ref[...], preferred_element_type=F32)
    o_ref[...] = jax.nn.sigmoid(acc).astype(o_ref.dtype)


def _proj_gate(xn, w, layer, *, tm, tn):
    T, D = xn.shape
    N = w.shape[2]
    return pl.pallas_call(
        _proj_gate_kernel,
        grid=(T // tm, N // tn),
        in_specs=[
            pl.BlockSpec((tm, D), lambda i, j: (i, 0)),
            pl.BlockSpec((None, D, tn), lambda i, j: (layer, 0, j)),
        ],
        out_specs=pl.BlockSpec((tm, tn), lambda i, j: (i, j)),
        out_shape=jax.ShapeDtypeStruct((T, N), BF16),
        compiler_params=_compiler_params(("parallel", "arbitrary")),
        name="proj_gate",
    )(xn, w)


def _rotary(t, cos, sin):
    return t * cos + pltpu.roll(t, t.shape[1] // 2, axis=1) * sin


def _mixer_kernel(x_ref, w_ref, cos_ref, sin_ref, poolw_ref, pscale_ref, dw_ref, cb_ref, lng_ref, lnb_ref,
                  gng_ref, out_ref, u_scr, h_scr, state_scr, decay_scr, xi_scr, zeta_scr,
                  *, tm, chunk, tiles_per_seq, pool_width, conv_width, conv_kernel, ret_width):
    tile = pl.program_id(0)
    s = tile % tiles_per_seq
    heads = RET_HEADS
    dh = ret_width // heads
    pool_group = pool_width // len(POOL_WINDOWS)
    log_gamma = [math.log1p(-(2.0 ** (-5.0 - h))) for h in range(heads)]

    off_a = pool_width
    off_g = off_a + conv_width
    off_q = off_g + conv_width
    off_k = off_q + ret_width
    off_v = off_k + ret_width
    off_gate = off_v + ret_width

    def stream(c0, width):
        return jnp.dot(x_ref[...], w_ref[:, c0:c0 + width], preferred_element_type=F32)

    @pl.when(tile == 0)
    def _():
        row = lax.broadcasted_iota(jnp.int32, (chunk, chunk), 0).astype(F32)
        col = lax.broadcasted_iota(jnp.int32, (chunk, chunk), 1).astype(F32)
        rel = row - col
        for h in range(heads):
            decay_scr[h] = jnp.where(rel >= 0, jnp.exp(jnp.maximum(rel, 0.0) * log_gamma[h]), 0.0)
            xi_scr[h] = jnp.exp((row + 1.0) * log_gamma[h])
            zeta_scr[h] = jnp.exp((chunk - 1.0 - row) * log_gamma[h])

    @pl.when(s == 0)
    def _():
        u_scr[0:POOL_HALO, :] = jnp.zeros((POOL_HALO, pool_width), F32)
        h_scr[0:CONV_HALO, :] = jnp.zeros((CONV_HALO, conv_width), F32)
        state_scr[...] = jnp.zeros_like(state_scr)

    u_scr[POOL_HALO:POOL_HALO + tm, :] = stream(0, pool_width)
    t_idx = s * tm + lax.broadcasted_iota(jnp.int32, (tm, 1), 0)
    for gi, w in enumerate(POOL_WINDOWS):
        c0 = gi * pool_group
        ug = u_scr[:, c0:c0 + pool_group]
        win = ug
        span = 1
        while span < w:
            win = win + pltpu.roll(win, span, axis=0)
            span *= 2
        cnt = jnp.minimum(t_idx + 1, w).astype(F32)
        p = win[POOL_HALO:, :] / cnt - ug[POOL_HALO:, :]
        y = jnp.dot(p.astype(BF16), poolw_ref[gi], preferred_element_type=F32)
        out_ref[:, c0:c0 + pool_group] = (y * pscale_ref[:, c0:c0 + pool_group]).astype(out_ref.dtype)
    u_scr[0:POOL_HALO, :] = u_scr[tm:tm + POOL_HALO, :]

    h_scr[CONV_HALO:CONV_HALO + tm, :] = stream(off_a, conv_width) * jax.nn.sigmoid(stream(off_g, conv_width))
    win_rows = CONV_ROWS + SUBLANES
    for r0 in range(0, tm, CONV_ROWS):
        base = CONV_HALO + r0 - SUBLANES
        blocks = []
        for c0 in range(0, conv_width, LANES):
            cols = slice(c0, c0 + LANES)
            acc = None
            for r in range(SUBLANES):
                part = None
                for a in range((conv_kernel - 1 - r) // SUBLANES + 1):
                    j = conv_kernel - 1 - (SUBLANES * a + r)
                    term = dw_ref[j:j + 1, cols] * h_scr[base - SUBLANES * a:base - SUBLANES * a + win_rows, cols]
                    part = term if part is None else part + term
                if r:
                    part = pltpu.roll(part, r, axis=0)
                acc = part if acc is None else acc + part
            blocks.append(acc[SUBLANES:, :] + cb_ref[:, cols])
        conv = jnp.concatenate(blocks, axis=1)
        mu = jnp.mean(conv, axis=-1, keepdims=True)
        d = conv - mu
        var = jnp.mean(d * d, axis=-1, keepdims=True)
        y = d * lax.rsqrt(var + LN_EPS) * lng_ref[...] + lnb_ref[...]
        out_ref[r0:r0 + CONV_ROWS, pool_width:pool_width + conv_width] = (
            y * jax.nn.sigmoid(y)).astype(out_ref.dtype)
    h_scr[0:CONV_HALO, :] = h_scr[tm:tm + CONV_HALO, :]

    out_off = pool_width + conv_width
    cosf = cos_ref[...]
    sins = sin_ref[...]
    group = RET_HEAD_GROUP
    for h0 in range(0, heads, group):
        width = group * dh
        q_g = stream(off_q + h0 * dh, width)
        k_g = stream(off_k + h0 * dh, width)
        v_g = stream(off_v + h0 * dh, width)
        gate_g = stream(off_gate + h0 * dh, width)
        for hh in range(group):
            h = h0 + hh
            lanes = slice(hh * dh, (hh + 1) * dh)
            gamma_c = math.exp(chunk * log_gamma[h])
            qr = _rotary(q_g[:, lanes], cosf, sins).astype(BF16)
            kr = (_rotary(k_g[:, lanes], cosf, sins) * (dh ** -0.5)).astype(BF16)
            for c in range(tm // chunk):
                rows = slice(c * chunk, (c + 1) * chunk)
                q = qr[rows]
                k = kr[rows]
                v = v_g[rows, lanes]
                scores = lax.dot_general(q, k, (((1,), (1,)), ((), ())), preferred_element_type=F32)
                scores = scores * decay_scr[h]
                intra = jnp.dot(scores.astype(BF16), v.astype(BF16), preferred_element_type=F32)
                state = state_scr[h]
                inter = jnp.dot(q, state.astype(BF16), preferred_element_type=F32) * xi_scr[h]
                kv = lax.dot_general(k, (v * zeta_scr[h]).astype(BF16), (((0,), (0,)), ((), ())),
                                     preferred_element_type=F32)
                state_scr[h] = gamma_c * state + kv
                o = intra + inter
                mu = jnp.mean(o, axis=-1, keepdims=True)
                d = o - mu
                var = jnp.mean(d * d, axis=-1, keepdims=True)
                o = d * lax.rsqrt(var + LN_EPS) * gng_ref[:, h * dh:(h + 1) * dh]
                gate = gate_g[rows, lanes]
                out_ref[rows, out_off + h * dh:out_off + (h + 1) * dh] = (
                    gate * jax.nn.sigmoid(gate) * o).astype(out_ref.dtype)


def _mixers(xn, w_in16, layer, cosf, sins, pool_w, pool_scale, conv_dw, conv_b, ln_g, ln_b, gn_g,
            *, seq_len, tm, chunk):
    T, d_model = xn.shape
    pool_width = pool_scale.shape[-1]
    conv_kernel, conv_width = conv_dw.shape
    ret_width = gn_g.shape[-1]
    n_mix = pool_width + 2 * conv_width + 4 * ret_width
    dh = ret_width // RET_HEADS
    n_groups = len(POOL_WINDOWS)
    pool_group = pool_width // n_groups
    assert pool_width + conv_width + ret_width == d_model and n_mix <= w_in16.shape[2]
    assert dh == LANES and pool_group == LANES and conv_width % LANES == 0 and RET_HEADS % RET_HEAD_GROUP == 0
    assert chunk % SUBLANES == 0 and tm % chunk == 0 and tm % CONV_ROWS == 0 and seq_len % tm == 0
    assert max(POOL_WINDOWS) - 1 <= POOL_HALO and SUBLANES * ((conv_kernel - 1) // SUBLANES + 1) <= CONV_HALO

    def row(v):
        return v.reshape(1, -1)

    def const_spec(shape):
        return pl.BlockSpec(shape, lambda i: (0,) * len(shape))

    kern = functools.partial(_mixer_kernel, tm=tm, chunk=chunk, tiles_per_seq=seq_len // tm,
                             pool_width=pool_width, conv_width=conv_width, conv_kernel=conv_kernel,
                             ret_width=ret_width)
    return pl.pallas_call(
        kern,
        grid=(T // tm,),
        in_specs=[
            pl.BlockSpec((tm, d_model), lambda i: (i, 0)),
            pl.BlockSpec((None, d_model, n_mix), lambda i: (layer, 0, 0), pipeline_mode=pl.Buffered(1)),
            pl.BlockSpec((tm, dh), lambda i: (i, 0)),
            pl.BlockSpec((tm, dh), lambda i: (i, 0)),
            pl.BlockSpec((None, n_groups, pool_group, pool_group), lambda i: (layer, 0, 0, 0)),
            const_spec((1, pool_width)),
            const_spec((conv_kernel, conv_width)),
            const_spec((1, conv_width)),
            const_spec((1, conv_width)),
            const_spec((1, conv_width)),
            const_spec((1, ret_width)),
        ],
        out_specs=pl.BlockSpec((tm, d_model), lambda i: (i, 0)),
        out_shape=jax.ShapeDtypeStruct((T, d_model), BF16),
        scratch_shapes=[
            pltpu.VMEM((POOL_HALO + tm, pool_width), F32),
            pltpu.VMEM((CONV_HALO + tm, conv_width), F32),
            pltpu.VMEM((RET_HEADS, dh, dh), F32),
            pltpu.VMEM((RET_HEADS, chunk, chunk), F32),
            pltpu.VMEM((RET_HEADS, chunk, dh), F32),
            pltpu.VMEM((RET_HEADS, chunk, dh), F32),
        ],
        compiler_params=_compiler_params(("arbitrary",)),
        name="mixers",
    )(xn, w_in16, cosf, sins, pool_w, row(pool_scale), conv_dw, row(conv_b), row(ln_g), row(ln_b), row(gn_g))


def _merge_kernel(br_ref, gate_ref, x_ref, wp_ref, wc_ref, wr_ref, wo_ref, g_ref, gn_ref, o_ref, xn_ref,
                  *, pool_width, conv_width):
    d = x_ref.shape[-1]
    c1 = pool_width
    c2 = pool_width + conv_width
    y_pool = jnp.dot(br_ref[:, 0:c1], wp_ref[...], preferred_element_type=F32)
    merged = gate_ref[:, 0:d].astype(F32) * y_pool
    y_conv = jnp.dot(br_ref[:, c1:c2], wc_ref[...], preferred_element_type=F32)
    merged = merged + gate_ref[:, d:2 * d].astype(F32) * y_conv
    y_ret = jnp.dot(br_ref[:, c2:d], wr_ref[...], preferred_element_type=F32)
    merged = merged + gate_ref[:, 2 * d:3 * d].astype(F32) * y_ret
    out = jnp.dot(merged.astype(BF16), wo_ref[...], preferred_element_type=F32)
    x_new = x_ref[...] + _rms_norm(out, g_ref[...])
    o_ref[...] = x_new
    xn_ref[...] = _rms_norm(x_new, gn_ref[...]).astype(xn_ref.dtype)


def _merge(br, gates, x, wp, wc, wr, wo, layer, g_post, g_next, *, tm):
    T, D = x.shape
    pool_width, conv_width, ret_width = wp.shape[1], wc.shape[1], wr.shape[1]

    def resident(rows):
        return pl.BlockSpec((None, rows, D), lambda i: (layer, 0, 0), pipeline_mode=pl.Buffered(1))

    def vec():
        return pl.BlockSpec((1, D), lambda i: (0, 0))

    def rows(width):
        return pl.BlockSpec((tm, width), lambda i: (i, 0))

    return pl.pallas_call(
        functools.partial(_merge_kernel, pool_width=pool_width, conv_width=conv_width),
        grid=(T // tm,),
        in_specs=[rows(D), rows(N_BRANCH * D), rows(D), resident(pool_width), resident(conv_width),
                  resident(ret_width), resident(D), vec(), vec()],
        out_specs=[rows(D), rows(D)],
        out_shape=[jax.ShapeDtypeStruct((T, D), F32), jax.ShapeDtypeStruct((T, D), BF16)],
        compiler_params=_compiler_params(("parallel",)),
        name="merge",
    )(br, gates, x, wp, wc, wr, wo, g_post, g_next)


def _ffn_kernel(xn_ref, x_ref, wa_ref, wb_ref, wo_ref, gpost_ref, *rest, emit_next):
    if emit_next:
        gnext_ref, o_ref, xn_out_ref = rest
    else:
        (o_ref,) = rest
    j = pl.program_id(1)

    @pl.when(j == 0)
    def _():
        o_ref[...] = jnp.zeros_like(o_ref)

    xn = xn_ref[...]
    a = jnp.dot(xn, wa_ref[...], preferred_element_type=F32)
    b = jnp.dot(xn, wb_ref[...], preferred_element_type=F32)
    hid = (a * jax.nn.sigmoid(a) * b).astype(BF16)
    o_ref[...] += jnp.dot(hid, wo_ref[...], preferred_element_type=F32)

    @pl.when(j == pl.num_programs(1) - 1)
    def _():
        x_new = x_ref[...] + _rms_norm(o_ref[...], gpost_ref[...])
        o_ref[...] = x_new
        if emit_next:
            xn_out_ref[...] = _rms_norm(x_new, gnext_ref[...]).astype(xn_out_ref.dtype)


def _ffn(xn, x, w_in, w_out, layer, g_post, g_next, *, tm, th):
    T, D = x.shape
    hidden = w_out.shape[1]
    nh = hidden // th
    emit_next = g_next is not None

    def rows():
        return pl.BlockSpec((tm, D), lambda i, j: (i, 0))

    def vec():
        return pl.BlockSpec((1, D), lambda i, j: (0, 0))

    in_specs = [
        rows(), rows(),
        pl.BlockSpec((None, D, th), lambda i, j: (layer, 0, j)),
        pl.BlockSpec((None, D, th), lambda i, j: (layer, 0, j + nh)),
        pl.BlockSpec((None, th, D), lambda i, j: (layer, j, 0)),
        vec(),
    ]
    args = [xn, x, w_in, w_in, w_out, g_post]
    out_specs = [rows()]
    out_shape = [jax.ShapeDtypeStruct((T, D), F32)]
    if emit_next:
        in_specs.append(vec())
        args.append(g_next)
        out_specs.append(rows())
        out_shape.append(jax.ShapeDtypeStruct((T, D), BF16))
    outs = pl.pallas_call(
        functools.partial(_ffn_kernel, emit_next=emit_next),
        grid=(T // tm, nh),
        in_specs=in_specs,
        out_specs=out_specs,
        out_shape=out_shape,
        compiler_params=_compiler_params(("parallel", "arbitrary")),
        name="ffn",
    )(*args)
    return (outs[0], outs[1]) if emit_next else (outs[0], None)


def kernel(x, positions, g_mix_pre, g_mix_post, g_ffn_pre, g_ffn_post, w_in, pool_w, pool_scale, conv_dw,
           conv_b, conv_ln_g, conv_ln_b, ret_gn_g, w_pool_proj, w_conv_proj, w_ret_proj, w_out, w_ffn_in,
           w_ffn_out):
    B, S, D = x.shape
    depth = w_in.shape[0]
    T = B * S
    pool_width = pool_scale.shape[-1]
    conv_width = conv_b.shape[-1]
    ret_width = ret_gn_g.shape[-1]
    n_mix = pool_width + 2 * conv_width + 4 * ret_width
    hidden = w_ffn_out.shape[1]
    head_dim = ret_width // RET_HEADS
    chunk = 128

    tm_mix = _pick(S, (256, 128))
    tm_gate = _pick(T, (2048, 1024, 512, 256, 128))
    tn_gate = _pick(N_BRANCH * D, (1024, 512, 256, 128))
    tm_merge = _pick(T, (256, 128))
    tm_ffn = _pick(T, (512, 256, 128))
    th_ffn = _pick(hidden, (512, 256, 128))

    w_in16 = w_in.astype(BF16)
    w_gate = w_in16[:, :, n_mix:]
    pool_w16 = pool_w.astype(BF16)
    wp16 = w_pool_proj.astype(BF16)
    wc16 = w_conv_proj.astype(BF16)
    wr16 = w_ret_proj.astype(BF16)
    wo16 = w_out.astype(BF16)
    wfi16 = w_ffn_in.astype(BF16)
    wfo16 = w_ffn_out.astype(BF16)

    cosf, sins = _rope_tables(positions, head_dim)

    xf = x.reshape(T, D)
    xn = _norm(xf, g_mix_pre[0].reshape(1, D))
    for l in range(depth):
        br = _mixers(xn, w_in16, l, cosf, sins, pool_w16, pool_scale[l], conv_dw[l], conv_b[l], conv_ln_g[l],
                     conv_ln_b[l], ret_gn_g[l], seq_len=S, tm=tm_mix, chunk=chunk)
        gates = _proj_gate(xn, w_gate, l, tm=tm_gate, tn=tn_gate)
        xf, xn = _merge(br, gates, xf, wp16, wc16, wr16, wo16, l,
                        g_mix_post[l].reshape(1, D), g_ffn_pre[l].reshape(1, D), tm=tm_merge)
        g_next = g_mix_pre[l + 1].reshape(1, D) if l + 1 < depth else None
        xf, xn = _ffn(xn, xf, wfi16, wfo16, l, g_ffn_post[l].reshape(1, D), g_next, tm=tm_ffn, th=th_ffn)
    return xf.reshape(B, S, D)
```

```python
import functools
import math

import jax
import jax.numpy as jnp
from jax import lax
from jax.experimental import pallas as pl
from jax.experimental.pallas import tpu as pltpu

F32 = jnp.float32
BF16 = jnp.bfloat16

NORM_EPS = 1e-6
LN_EPS = 1e-5
ROPE_BASE = 10000.0
POOL_WINDOWS = (2, 4, 8, 16)
RET_HEADS = 8
N_BRANCH = 3

LANES = 128
SUBLANES = 8
BF16_ROWS = 16
VMEM_LIMIT_BYTES = 56 * 1024 * 1024

POOL_HALO = 16
CONV_HALO = 32
CONV_ROWS = 64
RET_HEAD_GROUP = 4


def _rms_norm(x, g):
    return x * lax.rsqrt(jnp.mean(x * x, axis=-1, keepdims=True) + NORM_EPS) * g


def _compiler_params(semantics):
    return pltpu.CompilerParams(dimension_semantics=semantics, vmem_limit_bytes=VMEM_LIMIT_BYTES)


def _pick(n, candidates):
    for c in candidates:
        if n % c == 0:
            return c
    raise ValueError(f"no tile in {candidates} divides {n}")


def _rope_kernel(pos_ref, cos_ref, sin_ref, *, half):
    lane = lax.broadcasted_iota(jnp.int32, (1, 2 * half), 1)
    freq = (lane % half).astype(F32)
    inv = jnp.exp(freq * (-math.log(ROPE_BASE) / half))
    ang = pos_ref[...] * inv
    cos_ref[...] = jnp.cos(ang)
    sin_ref[...] = jnp.where(lane < half, -jnp.sin(ang), jnp.sin(ang))


def _rope_tables(positions, head_dim):
    B, S = positions.shape
    T = B * S
    tm = _pick(T, (512, 256, 128))
    posb = jnp.broadcast_to(positions.astype(F32).reshape(T, 1), (T, head_dim))
    spec = pl.BlockSpec((tm, head_dim), lambda i: (i, 0))
    return pl.pallas_call(
        functools.partial(_rope_kernel, half=head_dim // 2),
        grid=(T // tm,),
        in_specs=[spec],
        out_specs=[spec, spec],
        out_shape=[jax.ShapeDtypeStruct((T, head_dim), F32)] * 2,
        compiler_params=_compiler_params(("parallel",)),
        name="rope_tables",
    )(posb)


def _cast_specs(w, layer, n_steps, period=1):
    _, R, C = w.shape
    rows = R * period // n_steps
    assert R * period % n_steps == 0 and rows % BF16_ROWS == 0 and C % LANES == 0
    in_spec = pl.BlockSpec((None, rows, C), lambda i: (layer, i // period, 0))
    out_spec = pl.BlockSpec((rows, C), lambda i: (i // period, 0))
    return in_spec, out_spec, jax.ShapeDtypeStruct((R, C), BF16)


def _split_cast_specs(w, layer, n_steps, n_first):
    _, R, C = w.shape
    rows = R // n_steps
    assert R % n_steps == 0 and rows % BF16_ROWS == 0 and n_first % LANES == 0 and C % LANES == 0
    in_spec = pl.BlockSpec((None, rows, C), lambda i: (layer, i, 0))
    out_specs = [pl.BlockSpec((rows, n_first), lambda i: (i, 0)), pl.BlockSpec((rows, C - n_first), lambda i: (i, 0))]
    out_shapes = [jax.ShapeDtypeStruct((R, n_first), BF16), jax.ShapeDtypeStruct((R, C - n_first), BF16)]
    return in_spec, out_specs, out_shapes


def _split_cast(w_ref, first_ref, second_ref):
    n_first = first_ref.shape[1]
    first_ref[...] = w_ref[:, 0:n_first].astype(first_ref.dtype)
    second_ref[...] = w_ref[:, n_first:].astype(second_ref.dtype)


def _norm_kernel(x_ref, g_ref, w_ref, o_ref, wmix_ref, wgate_ref):
    o_ref[...] = _rms_norm(x_ref[...], g_ref[...]).astype(o_ref.dtype)
    _split_cast(w_ref, wmix_ref, wgate_ref)


def _norm(x, g, w_in, n_mix):
    T, D = x.shape
    tm = _pick(T, (512, 256, 128))
    w_spec, w_out_specs, w_out_shapes = _split_cast_specs(w_in, 0, T // tm, n_mix)
    return pl.pallas_call(
        _norm_kernel,
        grid=(T // tm,),
        in_specs=[pl.BlockSpec((tm, D), lambda i: (i, 0)), pl.BlockSpec((1, D), lambda i: (0, 0)), w_spec],
        out_specs=[pl.BlockSpec((tm, D), lambda i: (i, 0))] + w_out_specs,
        out_shape=[jax.ShapeDtypeStruct((T, D), BF16)] + w_out_shapes,
        compiler_params=_compiler_params(("parallel",)),
        name="norm_in",
    )(x, g, w_in)


def _proj_gate_kernel(x_ref, w_ref, o_ref):
    acc = jnp.dot(x_ref[...], w_ref[...], preferred_element_type=F32)
    o_ref[...] = jax.nn.sigmoid(acc).astype(o_ref.dtype)


def _proj_gate(xn, w, *, tm, tn):
    T, D = xn.shape
    N = w.shape[1]
    return pl.pallas_call(
        _proj_gate_kernel,
        grid=(T // tm, N // tn),
        in_specs=[
            pl.BlockSpec((tm, D), lambda i, j: (i, 0)),
            pl.BlockSpec((D, tn), lambda i, j: (0, j)),
        ],
        out_specs=pl.BlockSpec((tm, tn), lambda i, j: (i, j)),
        out_shape=jax.ShapeDtypeStruct((T, N), BF16),
        compiler_params=_compiler_params(("parallel", "arbitrary")),
        name="proj_gate",
    )(xn, w)


def _rotary(t, cos, sin):
    return t * cos + pltpu.roll(t, t.shape[1] // 2, axis=1) * sin


def _mixer_kernel(x_ref, w_ref, cos_ref, sin_ref, poolw_ref, pscale_ref, dw_ref, cb_ref, lng_ref, lnb_ref,
                  gng_ref, wfi_ref, wfo_ref, out_ref, wfi16_ref, wfo16_ref,
                  u_scr, h_scr, state_scr, decay_scr, xi_scr, zeta_scr,
                  *, tm, chunk, tiles_per_seq, pool_width, conv_width, conv_kernel, ret_width):
    wfi16_ref[...] = wfi_ref[...].astype(wfi16_ref.dtype)
    wfo16_ref[...] = wfo_ref[...].astype(wfo16_ref.dtype)

    tile = pl.program_id(0)
    s = tile % tiles_per_seq
    heads = RET_HEADS
    dh = ret_width // heads
    pool_group = pool_width // len(POOL_WINDOWS)
    log_gamma = [math.log1p(-(2.0 ** (-5.0 - h))) for h in range(heads)]

    off_a = pool_width
    off_g = off_a + conv_width
    off_q = off_g + conv_width
    off_k = off_q + ret_width
    off_v = off_k + ret_width
    off_gate = off_v + ret_width

    def stream(c0, width):
        return jnp.dot(x_ref[...], w_ref[:, c0:c0 + width], preferred_element_type=F32)

    @pl.when(tile == 0)
    def _():
        row = lax.broadcasted_iota(jnp.int32, (chunk, chunk), 0).astype(F32)
        col = lax.broadcasted_iota(jnp.int32, (chunk, chunk), 1).astype(F32)
        rel = row - col
        for h in range(heads):
            decay_scr[h] = jnp.where(rel >= 0, jnp.exp(jnp.maximum(rel, 0.0) * log_gamma[h]), 0.0)
            xi_scr[h] = jnp.exp((row + 1.0) * log_gamma[h])
            zeta_scr[h] = jnp.exp((chunk - 1.0 - row) * log_gamma[h])

    @pl.when(s == 0)
    def _():
        u_scr[0:POOL_HALO, :] = jnp.zeros((POOL_HALO, pool_width), F32)
        h_scr[0:CONV_HALO, :] = jnp.zeros((CONV_HALO, conv_width), F32)
        state_scr[...] = jnp.zeros_like(state_scr)

    u_scr[POOL_HALO:POOL_HALO + tm, :] = stream(0, pool_width)
    t_idx = s * tm + lax.broadcasted_iota(jnp.int32, (tm, 1), 0)
    for gi, w in enumerate(POOL_WINDOWS):
        c0 = gi * pool_group
        ug = u_scr[:, c0:c0 + pool_group]
        win = ug
        span = 1
        while span < w:
            win = win + pltpu.roll(win, span, axis=0)
            span *= 2
        cnt = jnp.minimum(t_idx + 1, w).astype(F32)
        p = win[POOL_HALO:, :] / cnt - ug[POOL_HALO:, :]
        y = jnp.dot(p.astype(BF16), poolw_ref[gi].astype(BF16), preferred_element_type=F32)
        out_ref[:, c0:c0 + pool_group] = (y * pscale_ref[:, c0:c0 + pool_group]).astype(out_ref.dtype)
    u_scr[0:POOL_HALO, :] = u_scr[tm:tm + POOL_HALO, :]

    h_scr[CONV_HALO:CONV_HALO + tm, :] = stream(off_a, conv_width) * jax.nn.sigmoid(stream(off_g, conv_width))
    win_rows = CONV_ROWS + SUBLANES
    for r0 in range(0, tm, CONV_ROWS):
        base = CONV_HALO + r0 - SUBLANES
        blocks = []
        for c0 in range(0, conv_width, LANES):
            cols = slice(c0, c0 + LANES)
            acc = None
            for r in range(SUBLANES):
                part = None
                for a in range((conv_kernel - 1 - r) // SUBLANES + 1):
                    j = conv_kernel - 1 - (SUBLANES * a + r)
                    term = dw_ref[j:j + 1, cols] * h_scr[base - SUBLANES * a:base - SUBLANES * a + win_rows, cols]
                    part = term if part is None else part + term
                if r:
                    part = pltpu.roll(part, r, axis=0)
                acc = part if acc is None else acc + part
            blocks.append(acc[SUBLANES:, :] + cb_ref[:, cols])
        conv = jnp.concatenate(blocks, axis=1)
        mu = jnp.mean(conv, axis=-1, keepdims=True)
        d = conv - mu
        var = jnp.mean(d * d, axis=-1, keepdims=True)
        y = d * lax.rsqrt(var + LN_EPS) * lng_ref[...] + lnb_ref[...]
        out_ref[r0:r0 + CONV_ROWS, pool_width:pool_width + conv_width] = (
            y * jax.nn.sigmoid(y)).astype(out_ref.dtype)
    h_scr[0:CONV_HALO, :] = h_scr[tm:tm + CONV_HALO, :]

    out_off = pool_width + conv_width
    cosf = cos_ref[...]
    sins = sin_ref[...]
    group = RET_HEAD_GROUP
    for h0 in range(0, heads, group):
        width = group * dh
        q_g = stream(off_q + h0 * dh, width)
        k_g = stream(off_k + h0 * dh, width)
        v_g = stream(off_v + h0 * dh, width)
        gate_g = stream(off_gate + h0 * dh, width)
        for hh in range(group):
            h = h0 + hh
            lanes = slice(hh * dh, (hh + 1) * dh)
            gamma_c = math.exp(chunk * log_gamma[h])
            qr = _rotary(q_g[:, lanes], cosf, sins).astype(BF16)
            kr = (_rotary(k_g[:, lanes], cosf, sins) * (dh ** -0.5)).astype(BF16)
            for c in range(tm // chunk):
                rows = slice(c * chunk, (c + 1) * chunk)
                q = qr[rows]
                k = kr[rows]
                v = v_g[rows, lanes]
                scores = lax.dot_general(q, k, (((1,), (1,)), ((), ())), preferred_element_type=F32)
                scores = scores * decay_scr[h]
                intra = jnp.dot(scores.astype(BF16), v.astype(BF16), preferred_element_type=F32)
                state = state_scr[h]
                inter = jnp.dot(q, state.astype(BF16), preferred_element_type=F32) * xi_scr[h]
                kv = lax.dot_general(k, (v * zeta_scr[h]).astype(BF16), (((0,), (0,)), ((), ())),
                                     preferred_element_type=F32)
                state_scr[h] = gamma_c * state + kv
                o = intra + inter
                mu = jnp.mean(o, axis=-1, keepdims=True)
                d = o - mu
                var = jnp.mean(d * d, axis=-1, keepdims=True)
                o = d * lax.rsqrt(var + LN_EPS) * gng_ref[:, h * dh:(h + 1) * dh]
                gate = gate_g[rows, lanes]
                out_ref[rows, out_off + h * dh:out_off + (h + 1) * dh] = (
                    gate * jax.nn.sigmoid(gate) * o).astype(out_ref.dtype)


def _mixers(xn, w_mix, layer, cosf, sins, pool_w, pool_scale, conv_dw, conv_b, ln_g, ln_b, gn_g, w_ffn_in,
            w_ffn_out, *, seq_len, tm, chunk):
    T, d_model = xn.shape
    pool_width = pool_scale.shape[-1]
    conv_kernel, conv_width = conv_dw.shape
    ret_width = gn_g.shape[-1]
    n_mix = pool_width + 2 * conv_width + 4 * ret_width
    dh = ret_width // RET_HEADS
    n_groups = len(POOL_WINDOWS)
    pool_group = pool_width // n_groups
    n_steps = T // tm
    assert pool_width + conv_width + ret_width == d_model and w_mix.shape == (d_model, n_mix)
    assert dh == LANES and pool_group == LANES and conv_width % LANES == 0 and RET_HEADS % RET_HEAD_GROUP == 0
    assert chunk % SUBLANES == 0 and tm % chunk == 0 and tm % CONV_ROWS == 0 and seq_len % tm == 0
    assert max(POOL_WINDOWS) - 1 <= POOL_HALO and SUBLANES * ((conv_kernel - 1) // SUBLANES + 1) <= CONV_HALO

    def row(v):
        return v.reshape(1, -1)

    def const_spec(shape):
        return pl.BlockSpec(shape, lambda i: (0,) * len(shape))

    wfi_in, wfi_out, wfi_shape = _cast_specs(w_ffn_in, layer, n_steps)
    wfo_period = 1 if (w_ffn_out.shape[1] // n_steps) % BF16_ROWS == 0 else 2
    wfo_in, wfo_out, wfo_shape = _cast_specs(w_ffn_out, layer, n_steps, wfo_period)

    kern = functools.partial(_mixer_kernel, tm=tm, chunk=chunk, tiles_per_seq=seq_len // tm,
                             pool_width=pool_width, conv_width=conv_width, conv_kernel=conv_kernel,
                             ret_width=ret_width)
    return pl.pallas_call(
        kern,
        grid=(n_steps,),
        in_specs=[
            pl.BlockSpec((tm, d_model), lambda i: (i, 0)),
            pl.BlockSpec((d_model, n_mix), lambda i: (0, 0), pipeline_mode=pl.Buffered(1)),
            pl.BlockSpec((tm, dh), lambda i: (i, 0)),
            pl.BlockSpec((tm, dh), lambda i: (i, 0)),
            pl.BlockSpec((None, n_groups, pool_group, pool_group), lambda i: (layer, 0, 0, 0)),
            const_spec((1, pool_width)),
            const_spec((conv_kernel, conv_width)),
            const_spec((1, conv_width)),
            const_spec((1, conv_width)),
            const_spec((1, conv_width)),
            const_spec((1, ret_width)),
            wfi_in,
            wfo_in,
        ],
        out_specs=[pl.BlockSpec((tm, d_model), lambda i: (i, 0)), wfi_out, wfo_out],
        out_shape=[jax.ShapeDtypeStruct((T, d_model), BF16), wfi_shape, wfo_shape],
        scratch_shapes=[
            pltpu.VMEM((POOL_HALO + tm, pool_width), F32),
            pltpu.VMEM((CONV_HALO + tm, conv_width), F32),
            pltpu.VMEM((RET_HEADS, dh, dh), F32),
            pltpu.VMEM((RET_HEADS, chunk, chunk), F32),
            pltpu.VMEM((RET_HEADS, chunk, dh), F32),
            pltpu.VMEM((RET_HEADS, chunk, dh), F32),
        ],
        compiler_params=_compiler_params(("arbitrary",)),
        name="mixers",
    )(xn, w_mix, cosf, sins, pool_w, row(pool_scale), conv_dw, row(conv_b), row(ln_g), row(ln_b), row(gn_g),
      w_ffn_in, w_ffn_out)


def _merge_kernel(br_ref, gate_ref, x_ref, wp_ref, wc_ref, wr_ref, wo_ref, g_ref, gn_ref, *rest,
                  pool_width, conv_width, cast_next):
    if cast_next:
        w_next_ref, o_ref, xn_ref, wmix_ref, wgate_ref = rest
        _split_cast(w_next_ref, wmix_ref, wgate_ref)
    else:
        o_ref, xn_ref = rest
    d = x_ref.shape[-1]
    c1 = pool_width
    c2 = pool_width + conv_width
    y_pool = jnp.dot(br_ref[:, 0:c1], wp_ref[...], preferred_element_type=F32)
    merged = gate_ref[:, 0:d].astype(F32) * y_pool
    y_conv = jnp.dot(br_ref[:, c1:c2], wc_ref[...], preferred_element_type=F32)
    merged = merged + gate_ref[:, d:2 * d].astype(F32) * y_conv
    y_ret = jnp.dot(br_ref[:, c2:d], wr_ref[...], preferred_element_type=F32)
    merged = merged + gate_ref[:, 2 * d:3 * d].astype(F32) * y_ret
    out = jnp.dot(merged.astype(BF16), wo_ref[...], preferred_element_type=F32)
    x_new = x_ref[...] + _rms_norm(out, g_ref[...])
    o_ref[...] = x_new
    xn_ref[...] = _rms_norm(x_new, gn_ref[...]).astype(xn_ref.dtype)


def _merge(br, gates, x, wp, wc, wr, wo, layer, g_post, g_next, w_in, n_mix, *, tm):
    T, D = x.shape
    pool_width, conv_width, ret_width = wp.shape[1], wc.shape[1], wr.shape[1]
    cast_next = layer + 1 < w_in.shape[0]

    def resident(rows):
        return pl.BlockSpec((None, rows, D), lambda i: (layer, 0, 0), pipeline_mode=pl.Buffered(1))

    def vec():
        return pl.BlockSpec((1, D), lambda i: (0, 0))

    def rows(width):
        return pl.BlockSpec((tm, width), lambda i: (i, 0))

    in_specs = [rows(D), rows(N_BRANCH * D), rows(D), resident(pool_width), resident(conv_width),
                resident(ret_width), resident(D), vec(), vec()]
    args = [br, gates, x, wp, wc, wr, wo, g_post, g_next]
    out_specs = [rows(D), rows(D)]
    out_shape = [jax.ShapeDtypeStruct((T, D), F32), jax.ShapeDtypeStruct((T, D), BF16)]
    if cast_next:
        w_spec, w_out_specs, w_out_shapes = _split_cast_specs(w_in, layer + 1, T // tm, n_mix)
        in_specs.append(w_spec)
        args.append(w_in)
        out_specs += w_out_specs
        out_shape += w_out_shapes
    outs = pl.pallas_call(
        functools.partial(_merge_kernel, pool_width=pool_width, conv_width=conv_width, cast_next=cast_next),
        grid=(T // tm,),
        in_specs=in_specs,
        out_specs=out_specs,
        out_shape=out_shape,
        compiler_params=_compiler_params(("parallel",)),
        name="merge",
    )(*args)
    return tuple(outs) if cast_next else (outs[0], outs[1], None, None)


def _ffn_kernel(xn_ref, x_ref, wa_ref, wb_ref, wo_ref, gpost_ref, *rest, emit_next):
    if emit_next:
        gnext_ref, o_ref, xn_out_ref = rest
    else:
        (o_ref,) = rest
    j = pl.program_id(1)

    @pl.when(j == 0)
    def _():
        o_ref[...] = jnp.zeros_like(o_ref)

    xn = xn_ref[...]
    a = jnp.dot(xn, wa_ref[...], preferred_element_type=F32)
    b = jnp.dot(xn, wb_ref[...], preferred_element_type=F32)
    hid = (a * jax.nn.sigmoid(a) * b).astype(BF16)
    o_ref[...] += jnp.dot(hid, wo_ref[...], preferred_element_type=F32)

    @pl.when(j == pl.num_programs(1) - 1)
    def _():
        x_new = x_ref[...] + _rms_norm(o_ref[...], gpost_ref[...])
        o_ref[...] = x_new
        if emit_next:
            xn_out_ref[...] = _rms_norm(x_new, gnext_ref[...]).astype(xn_out_ref.dtype)


def _ffn(xn, x, w_in, w_out, g_post, g_next, *, tm, th):
    T, D = x.shape
    hidden = w_out.shape[0]
    nh = hidden // th
    emit_next = g_next is not None

    def rows():
        return pl.BlockSpec((tm, D), lambda i, j: (i, 0))

    def vec():
        return pl.BlockSpec((1, D), lambda i, j: (0, 0))

    in_specs = [
        rows(), rows(),
        pl.BlockSpec((D, th), lambda i, j: (0, j)),
        pl.BlockSpec((D, th), lambda i, j: (0, j + nh)),
        pl.BlockSpec((th, D), lambda i, j: (j, 0)),
        vec(),
    ]
    args = [xn, x, w_in, w_in, w_out, g_post]
    out_specs = [rows()]
    out_shape = [jax.ShapeDtypeStruct((T, D), F32)]
    if emit_next:
        in_specs.append(vec())
        args.append(g_next)
        out_specs.append(rows())
        out_shape.append(jax.ShapeDtypeStruct((T, D), BF16))
    outs = pl.pallas_call(
        functools.partial(_ffn_kernel, emit_next=emit_next),
        grid=(T // tm, nh),
        in_specs=in_specs,
        out_specs=out_specs,
        out_shape=out_shape,
        compiler_params=_compiler_params(("parallel", "arbitrary")),
        name="ffn",
    )(*args)
    return (outs[0], outs[1]) if emit_next else (outs[0], None)


def kernel(x, positions, g_mix_pre, g_mix_post, g_ffn_pre, g_ffn_post, w_in, pool_w, pool_scale, conv_dw,
           conv_b, conv_ln_g, conv_ln_b, ret_gn_g, w_pool_proj, w_conv_proj, w_ret_proj, w_out, w_ffn_in,
           w_ffn_out):
    B, S, D = x.shape
    depth = w_in.shape[0]
    T = B * S
    pool_width = pool_scale.shape[-1]
    conv_width = conv_b.shape[-1]
    ret_width = ret_gn_g.shape[-1]
    n_mix = pool_width + 2 * conv_width + 4 * ret_width
    hidden = w_ffn_out.shape[1]
    head_dim = ret_width // RET_HEADS
    chunk = 128

    tm_mix = _pick(S, (256, 128))
    tm_gate = _pick(T, (2048, 1024, 512, 256, 128))
    tn_gate = _pick(N_BRANCH * D, (1024, 512, 256, 128))
    tm_merge = _pick(T, (256, 128))
    tm_ffn = _pick(T, (512, 256, 128))
    th_ffn = _pick(hidden, (512, 256, 128))

    wp16 = w_pool_proj.astype(BF16)
    wc16 = w_conv_proj.astype(BF16)
    wr16 = w_ret_proj.astype(BF16)
    wo16 = w_out.astype(BF16)

    cosf, sins = _rope_tables(positions, head_dim)

    xf = x.reshape(T, D)
    xn, w_mix, w_gate = _norm(xf, g_mix_pre[0].reshape(1, D), w_in, n_mix)
    for l in range(depth):
        br, wfi16, wfo16 = _mixers(xn, w_mix, l, cosf, sins, pool_w, pool_scale[l], conv_dw[l], conv_b[l],
                                   conv_ln_g[l], conv_ln_b[l], ret_gn_g[l], w_ffn_in, w_ffn_out,
                                   seq_len=S, tm=tm_mix, chunk=chunk)
        gates = _proj_gate(xn, w_gate, tm=tm_gate, tn=tn_gate)
        xf, xn, w_mix, w_gate = _merge(br, gates, xf, wp16, wc16, wr16, wo16, l, g_mix_post[l].reshape(1, D),
                                       g_ffn_pre[l].reshape(1, D), w_in, n_mix, tm=tm_merge)
        g_next = g_mix_pre[l + 1].reshape(1, D) if l + 1 < depth else None
        xf, xn = _ffn(xn, xf, wfi16, wfo16, g_ffn_post[l].reshape(1, D), g_next, tm=tm_ffn, th=th_ffn)
    return xf.reshape(B, S, D)
```

```python
import functools
import math

import jax
import jax.numpy as jnp
from jax import lax
from jax.experimental import pallas as pl
from jax.experimental.pallas import tpu as pltpu

F32 = jnp.float32
BF16 = jnp.bfloat16

NORM_EPS = 1e-6
LN_EPS = 1e-5
ROPE_BASE = 10000.0
POOL_WINDOWS = (2, 4, 8, 16)
RET_HEADS = 8
N_BRANCH = 3

LANES = 128
SUBLANES = 8
BF16_ROWS = 16
VMEM_LIMIT_BYTES = 56 * 1024 * 1024

POOL_HALO = 16
CONV_HALO = 32
CONV_ROWS = 64
RET_HEAD_GROUP = 4
GATE_ROWS = 512


def _rms_norm(x, g):
    return x * lax.rsqrt(jnp.mean(x * x, axis=-1, keepdims=True) + NORM_EPS) * g


def _sigmoid(x):
    return 0.5 * jnp.tanh(0.5 * x) + 0.5


def _compiler_params(semantics):
    return pltpu.CompilerParams(dimension_semantics=semantics, vmem_limit_bytes=VMEM_LIMIT_BYTES)


def _pick(n, candidates):
    for c in candidates:
        if n % c == 0:
            return c
    raise ValueError(f"no tile in {candidates} divides {n}")


def _rope(pos, half):
    lane = lax.broadcasted_iota(jnp.int32, (1, 2 * half), 1)
    freq = (lane % half).astype(F32)
    inv = jnp.exp(freq * (-math.log(ROPE_BASE) / half))
    ang = pos * inv
    sin = jnp.sin(ang)
    return jnp.cos(ang), jnp.where(lane < half, -sin, sin)


def _cast_specs(w, layer, n_steps, period=1):
    _, R, C = w.shape
    rows = R * period // n_steps
    assert R * period % n_steps == 0 and rows % BF16_ROWS == 0 and C % LANES == 0
    in_spec = pl.BlockSpec((None, rows, C), lambda i: (layer, i // period, 0))
    out_spec = pl.BlockSpec((rows, C), lambda i: (i // period, 0))
    return in_spec, out_spec, jax.ShapeDtypeStruct((R, C), BF16)


def _split_cast_specs(w, layer, n_steps, n_first):
    _, R, C = w.shape
    rows = R // n_steps
    assert R % n_steps == 0 and rows % BF16_ROWS == 0 and n_first % LANES == 0 and C % LANES == 0
    in_spec = pl.BlockSpec((None, rows, C), lambda i: (layer, i, 0))
    out_specs = [pl.BlockSpec((rows, n_first), lambda i: (i, 0)), pl.BlockSpec((rows, C - n_first), lambda i: (i, 0))]
    out_shapes = [jax.ShapeDtypeStruct((R, n_first), BF16), jax.ShapeDtypeStruct((R, C - n_first), BF16)]
    return in_spec, out_specs, out_shapes


def _split_cast(w_ref, first_ref, second_ref):
    n_first = first_ref.shape[1]
    first_ref[...] = w_ref[:, 0:n_first].astype(first_ref.dtype)
    second_ref[...] = w_ref[:, n_first:].astype(second_ref.dtype)


def _prologue_kernel(x_ref, g_ref, pos_ref, w_ref, o_ref, cos_ref, sin_ref, wmix_ref, wgate_ref):
    o_ref[...] = _rms_norm(x_ref[...], g_ref[...]).astype(o_ref.dtype)
    cos_ref[...], sin_ref[...] = _rope(pos_ref[...], pos_ref.shape[1] // 2)
    _split_cast(w_ref, wmix_ref, wgate_ref)


def _prologue(x, g, positions, head_dim, w_in, n_mix):
    T, D = x.shape
    tm = _pick(T, (512, 256, 128))
    posb = jnp.broadcast_to(positions.astype(F32).reshape(T, 1), (T, head_dim))
    w_spec, w_out_specs, w_out_shapes = _split_cast_specs(w_in, 0, T // tm, n_mix)
    rows = pl.BlockSpec((tm, D), lambda i: (i, 0))
    rope = pl.BlockSpec((tm, head_dim), lambda i: (i, 0))
    return pl.pallas_call(
        _prologue_kernel,
        grid=(T // tm,),
        in_specs=[rows, pl.BlockSpec((1, D), lambda i: (0, 0)), rope, w_spec],
        out_specs=[rows, rope, rope] + w_out_specs,
        out_shape=[jax.ShapeDtypeStruct((T, D), BF16)] + [jax.ShapeDtypeStruct((T, head_dim), F32)] * 2
        + w_out_shapes,
        compiler_params=_compiler_params(("parallel",)),
        name="prologue",
    )(x, g, posb, w_in)


def _proj_gate_kernel(x_ref, w_ref, o_ref):
    for r0 in range(0, x_ref.shape[0], GATE_ROWS):
        rows = slice(r0, min(r0 + GATE_ROWS, x_ref.shape[0]))
        acc = jnp.dot(x_ref[rows, :], w_ref[...], preferred_element_type=F32)
        o_ref[rows, :] = _sigmoid(acc).astype(o_ref.dtype)


def _proj_gate(xn, w, *, tm, tn):
    T, D = xn.shape
    N = w.shape[1]
    return pl.pallas_call(
        _proj_gate_kernel,
        grid=(T // tm, N // tn),
        in_specs=[
            pl.BlockSpec((tm, D), lambda i, j: (i, 0)),
            pl.BlockSpec((D, tn), lambda i, j: (0, j)),
        ],
        out_specs=pl.BlockSpec((tm, tn), lambda i, j: (i, j)),
        out_shape=jax.ShapeDtypeStruct((T, N), BF16),
        compiler_params=_compiler_params(("parallel", "arbitrary")),
        name="proj_gate",
    )(xn, w)


def _rotary(t, cos, sin):
    return t * cos + pltpu.roll(t, t.shape[1] // 2, axis=1) * sin


def _mixer_kernel(x_ref, w_ref, cos_ref, sin_ref, poolw_ref, pscale_ref, dw_ref, cb_ref, lng_ref, lnb_ref,
                  gng_ref, *rest, n_casts, tm, chunk, tiles_per_seq, pool_width, conv_width, conv_kernel,
                  ret_width):
    cast_in, out_ref, cast_out = rest[:n_casts], rest[n_casts], rest[n_casts + 1:2 * n_casts + 1]
    u_scr, h_scr, state_scr, decay_scr, xi_scr, zeta_scr = rest[2 * n_casts + 1:]
    for src_ref, dst_ref in zip(cast_in, cast_out):
        dst_ref[...] = src_ref[...].astype(dst_ref.dtype)

    tile = pl.program_id(0)
    s = tile % tiles_per_seq
    heads = RET_HEADS
    dh = ret_width // heads
    pool_group = pool_width // len(POOL_WINDOWS)
    log_gamma = [math.log1p(-(2.0 ** (-5.0 - h))) for h in range(heads)]

    off_a = pool_width
    off_g = off_a + conv_width
    off_q = off_g + conv_width
    off_k = off_q + ret_width
    off_v = off_k + ret_width
    off_gate = off_v + ret_width

    def stream(c0, width):
        return jnp.dot(x_ref[...], w_ref[:, c0:c0 + width], preferred_element_type=F32)

    @pl.when(tile == 0)
    def _():
        row = lax.broadcasted_iota(jnp.int32, (chunk, chunk), 0).astype(F32)
        col = lax.broadcasted_iota(jnp.int32, (chunk, chunk), 1).astype(F32)
        rel = row - col
        for h in range(heads):
            decay_scr[h] = jnp.where(rel >= 0, jnp.exp(jnp.maximum(rel, 0.0) * log_gamma[h]), 0.0)
            xi_scr[h] = jnp.exp((row + 1.0) * log_gamma[h])
            zeta_scr[h] = jnp.exp((chunk - 1.0 - row) * log_gamma[h])

    @pl.when(s == 0)
    def _():
        u_scr[0:POOL_HALO, :] = jnp.zeros((POOL_HALO, pool_width), F32)
        h_scr[0:CONV_HALO, :] = jnp.zeros((CONV_HALO, conv_width), F32)
        state_scr[...] = jnp.zeros_like(state_scr)

    u_scr[POOL_HALO:POOL_HALO + tm, :] = stream(0, pool_width)
    t_idx = s * tm + lax.broadcasted_iota(jnp.int32, (tm, 1), 0)
    for gi, w in enumerate(POOL_WINDOWS):
        c0 = gi * pool_group
        ug = u_scr[:, c0:c0 + pool_group]
        win = ug
        span = 1
        while span < w:
            win = win + pltpu.roll(win, span, axis=0)
            span *= 2
        cnt = jnp.minimum(t_idx + 1, w).astype(F32)
        p = win[POOL_HALO:, :] / cnt - ug[POOL_HALO:, :]
        y = jnp.dot(p.astype(BF16), poolw_ref[gi].astype(BF16), preferred_element_type=F32)
        out_ref[:, c0:c0 + pool_group] = (y * pscale_ref[:, c0:c0 + pool_group]).astype(out_ref.dtype)
    u_scr[0:POOL_HALO, :] = u_scr[tm:tm + POOL_HALO, :]

    h_scr[CONV_HALO:CONV_HALO + tm, :] = stream(off_a, conv_width) * _sigmoid(stream(off_g, conv_width))
    win_rows = CONV_ROWS + SUBLANES
    for r0 in range(0, tm, CONV_ROWS):
        base = CONV_HALO + r0 - SUBLANES
        blocks = []
        for c0 in range(0, conv_width, LANES):
            cols = slice(c0, c0 + LANES)
            acc = None
            for r in range(SUBLANES):
                part = None
                for a in range((conv_kernel - 1 - r) // SUBLANES + 1):
                    j = conv_kernel - 1 - (SUBLANES * a + r)
                    term = dw_ref[j:j + 1, cols] * h_scr[base - SUBLANES * a:base - SUBLANES * a + win_rows, cols]
                    part = term if part is None else part + term
                if r:
                    part = pltpu.roll(part, r, axis=0)
                acc = part if acc is None else acc + part
            blocks.append(acc[SUBLANES:, :] + cb_ref[:, cols])
        conv = jnp.concatenate(blocks, axis=1)
        mu = jnp.mean(conv, axis=-1, keepdims=True)
        d = conv - mu
        var = jnp.mean(d * d, axis=-1, keepdims=True)
        y = d * lax.rsqrt(var + LN_EPS) * lng_ref[...] + lnb_ref[...]
        out_ref[r0:r0 + CONV_ROWS, pool_width:pool_width + conv_width] = (
            y * _sigmoid(y)).astype(out_ref.dtype)
    h_scr[0:CONV_HALO, :] = h_scr[tm:tm + CONV_HALO, :]

    out_off = pool_width + conv_width
    cosf = cos_ref[...]
    sins = sin_ref[...]
    group = RET_HEAD_GROUP
    for h0 in range(0, heads, group):
        width = group * dh
        q_g = stream(off_q + h0 * dh, width)
        k_g = stream(off_k + h0 * dh, width)
        v_g = stream(off_v + h0 * dh, width)
        gate_g = stream(off_gate + h0 * dh, width)
        for hh in range(group):
            h = h0 + hh
            lanes = slice(hh * dh, (hh + 1) * dh)
            gamma_c = math.exp(chunk * log_gamma[h])
            qr = _rotary(q_g[:, lanes], cosf, sins).astype(BF16)
            kr = (_rotary(k_g[:, lanes], cosf, sins) * (dh ** -0.5)).astype(BF16)
            for c in range(tm // chunk):
                rows = slice(c * chunk, (c + 1) * chunk)
                q = qr[rows]
                k = kr[rows]
                v = v_g[rows, lanes]
                scores = lax.dot_general(q, k, (((1,), (1,)), ((), ())), preferred_element_type=F32)
                scores = scores * decay_scr[h]
                intra = jnp.dot(scores.astype(BF16), v.astype(BF16), preferred_element_type=F32)
                state = state_scr[h]
                inter = jnp.dot(q, state.astype(BF16), preferred_element_type=F32) * xi_scr[h]
                kv = lax.dot_general(k, (v * zeta_scr[h]).astype(BF16), (((0,), (0,)), ((), ())),
                                     preferred_element_type=F32)
                state_scr[h] = gamma_c * state + kv
                o = intra + inter
                mu = jnp.mean(o, axis=-1, keepdims=True)
                d = o - mu
                var = jnp.mean(d * d, axis=-1, keepdims=True)
                o = d * lax.rsqrt(var + LN_EPS) * gng_ref[:, h * dh:(h + 1) * dh]
                gate = gate_g[rows, lanes]
                out_ref[rows, out_off + h * dh:out_off + (h + 1) * dh] = (
                    gate * _sigmoid(gate) * o).astype(out_ref.dtype)


def _mixers(xn, w_mix, layer, cosf, sins, pool_w, pool_scale, conv_dw, conv_b, ln_g, ln_b, gn_g, cast_weights,
            *, seq_len, tm, chunk):
    T, d_model = xn.shape
    pool_width = pool_scale.shape[-1]
    conv_kernel, conv_width = conv_dw.shape
    ret_width = gn_g.shape[-1]
    n_mix = pool_width + 2 * conv_width + 4 * ret_width
    dh = ret_width // RET_HEADS
    n_groups = len(POOL_WINDOWS)
    pool_group = pool_width // n_groups
    n_steps = T // tm
    assert pool_width + conv_width + ret_width == d_model and w_mix.shape == (d_model, n_mix)
    assert dh == LANES and pool_group == LANES and conv_width % LANES == 0 and RET_HEADS % RET_HEAD_GROUP == 0
    assert chunk % SUBLANES == 0 and tm % chunk == 0 and tm % CONV_ROWS == 0 and seq_len % tm == 0
    assert max(POOL_WINDOWS) - 1 <= POOL_HALO and SUBLANES * ((conv_kernel - 1) // SUBLANES + 1) <= CONV_HALO

    def row(v):
        return v.reshape(1, -1)

    def const_spec(shape):
        return pl.BlockSpec(shape, lambda i: (0,) * len(shape))

    casts = []
    for w in cast_weights:
        period = 1
        while (w.shape[1] * period) % (n_steps * BF16_ROWS):
            period *= 2
        casts.append(_cast_specs(w, layer, n_steps, period))

    kern = functools.partial(_mixer_kernel, n_casts=len(casts), tm=tm, chunk=chunk, tiles_per_seq=seq_len // tm,
                             pool_width=pool_width, conv_width=conv_width, conv_kernel=conv_kernel,
                             ret_width=ret_width)
    return pl.pallas_call(
        kern,
        grid=(n_steps,),
        in_specs=[
            pl.BlockSpec((tm, d_model), lambda i: (i, 0)),
            pl.BlockSpec((d_model, n_mix), lambda i: (0, 0), pipeline_mode=pl.Buffered(1)),
            pl.BlockSpec((tm, dh), lambda i: (i, 0)),
            pl.BlockSpec((tm, dh), lambda i: (i, 0)),
            pl.BlockSpec((None, n_groups, pool_group, pool_group), lambda i: (layer, 0, 0, 0)),
            const_spec((1, pool_width)),
            const_spec((conv_kernel, conv_width)),
            const_spec((1, conv_width)),
            const_spec((1, conv_width)),
            const_spec((1, conv_width)),
            const_spec((1, ret_width)),
        ] + [c[0] for c in casts],
        out_specs=[pl.BlockSpec((tm, d_model), lambda i: (i, 0))] + [c[1] for c in casts],
        out_shape=[jax.ShapeDtypeStruct((T, d_model), BF16)] + [c[2] for c in casts],
        scratch_shapes=[
            pltpu.VMEM((POOL_HALO + tm, pool_width), F32),
            pltpu.VMEM((CONV_HALO + tm, conv_width), F32),
            pltpu.VMEM((RET_HEADS, dh, dh), F32),
            pltpu.VMEM((RET_HEADS, chunk, chunk), F32),
            pltpu.VMEM((RET_HEADS, chunk, dh), F32),
            pltpu.VMEM((RET_HEADS, chunk, dh), F32),
        ],
        compiler_params=_compiler_params(("arbitrary",)),
        name="mixers",
    )(xn, w_mix, cosf, sins, pool_w, row(pool_scale), conv_dw, row(conv_b), row(ln_g), row(ln_b), row(gn_g),
      *cast_weights)


def _merge_kernel(br_ref, gate_ref, x_ref, wp_ref, wc_ref, wr_ref, wo_ref, g_ref, gn_ref, *rest,
                  pool_width, conv_width, cast_next):
    if cast_next:
        w_next_ref, o_ref, xn_ref, wmix_ref, wgate_ref = rest
        _split_cast(w_next_ref, wmix_ref, wgate_ref)
    else:
        o_ref, xn_ref = rest
    d = x_ref.shape[-1]
    c1 = pool_width
    c2 = pool_width + conv_width
    y_pool = jnp.dot(br_ref[:, 0:c1], wp_ref[...], preferred_element_type=F32)
    merged = gate_ref[:, 0:d].astype(F32) * y_pool
    y_conv = jnp.dot(br_ref[:, c1:c2], wc_ref[...], preferred_element_type=F32)
    merged = merged + gate_ref[:, d:2 * d].astype(F32) * y_conv
    y_ret = jnp.dot(br_ref[:, c2:d], wr_ref[...], preferred_element_type=F32)
    merged = merged + gate_ref[:, 2 * d:3 * d].astype(F32) * y_ret
    out = jnp.dot(merged.astype(BF16), wo_ref[...], preferred_element_type=F32)
    x_new = x_ref[...] + _rms_norm(out, g_ref[...])
    o_ref[...] = x_new
    xn_ref[...] = _rms_norm(x_new, gn_ref[...]).astype(xn_ref.dtype)


def _merge(br, gates, x, wp, wc, wr, wo, layer, g_post, g_next, w_in, n_mix, *, tm):
    T, D = x.shape
    pool_width, conv_width, ret_width = wp.shape[0], wc.shape[0], wr.shape[0]
    cast_next = layer + 1 < w_in.shape[0]

    def resident(rows):
        return pl.BlockSpec((rows, D), lambda i: (0, 0), pipeline_mode=pl.Buffered(1))

    def vec():
        return pl.BlockSpec((1, D), lambda i: (0, 0))

    def rows(width):
        return pl.BlockSpec((tm, width), lambda i: (i, 0))

    in_specs = [rows(D), rows(N_BRANCH * D), rows(D), resident(pool_width), resident(conv_width),
                resident(ret_width), resident(D), vec(), vec()]
    args = [br, gates, x, wp, wc, wr, wo, g_post, g_next]
    out_specs = [rows(D), rows(D)]
    out_shape = [jax.ShapeDtypeStruct((T, D), F32), jax.ShapeDtypeStruct((T, D), BF16)]
    if cast_next:
        w_spec, w_out_specs, w_out_shapes = _split_cast_specs(w_in, layer + 1, T // tm, n_mix)
        in_specs.append(w_spec)
        args.append(w_in)
        out_specs += w_out_specs
        out_shape += w_out_shapes
    outs = pl.pallas_call(
        functools.partial(_merge_kernel, pool_width=pool_width, conv_width=conv_width, cast_next=cast_next),
        grid=(T // tm,),
        in_specs=in_specs,
        out_specs=out_specs,
        out_shape=out_shape,
        compiler_params=_compiler_params(("parallel",)),
        name="merge",
    )(*args)
    return tuple(outs) if cast_next else (outs[0], outs[1], None, None)


def _ffn_kernel(xn_ref, x_ref, wa_ref, wb_ref, wo_ref, gpost_ref, *rest, emit_next):
    if emit_next:
        gnext_ref, o_ref, xn_out_ref = rest
    else:
        (o_ref,) = rest
    j = pl.program_id(1)

    @pl.when(j == 0)
    def _():
        o_ref[...] = jnp.zeros_like(o_ref)

    xn = xn_ref[...]
    a = jnp.dot(xn, wa_ref[...], preferred_element_type=F32)
    b = jnp.dot(xn, wb_ref[...], preferred_element_type=F32)
    hid = (a * _sigmoid(a) * b).astype(BF16)
    o_ref[...] += jnp.dot(hid, wo_ref[...], preferred_element_type=F32)

    @pl.when(j == pl.num_programs(1) - 1)
    def _():
        x_new = x_ref[...] + _rms_norm(o_ref[...], gpost_ref[...])
        o_ref[...] = x_new
        if emit_next:
            xn_out_ref[...] = _rms_norm(x_new, gnext_ref[...]).astype(xn_out_ref.dtype)


def _ffn(xn, x, w_in, w_out, g_post, g_next, *, tm, th):
    T, D = x.shape
    hidden = w_out.shape[0]
    nh = hidden // th
    emit_next = g_next is not None

    def rows():
        return pl.BlockSpec((tm, D), lambda i, j: (i, 0))

    def vec():
        return pl.BlockSpec((1, D), lambda i, j: (0, 0))

    in_specs = [
        rows(), rows(),
        pl.BlockSpec((D, th), lambda i, j: (0, j)),
        pl.BlockSpec((D, th), lambda i, j: (0, j + nh)),
        pl.BlockSpec((th, D), lambda i, j: (j, 0)),
        vec(),
    ]
    args = [xn, x, w_in, w_in, w_out, g_post]
    out_specs = [rows()]
    out_shape = [jax.ShapeDtypeStruct((T, D), F32)]
    if emit_next:
        in_specs.append(vec())
        args.append(g_next)
        out_specs.append(rows())
        out_shape.append(jax.ShapeDtypeStruct((T, D), BF16))
    outs = pl.pallas_call(
        functools.partial(_ffn_kernel, emit_next=emit_next),
        grid=(T // tm, nh),
        in_specs=in_specs,
        out_specs=out_specs,
        out_shape=out_shape,
        compiler_params=_compiler_params(("parallel", "arbitrary")),
        name="ffn",
    )(*args)
    return (outs[0], outs[1]) if emit_next else (outs[0], None)


def kernel(x, positions, g_mix_pre, g_mix_post, g_ffn_pre, g_ffn_post, w_in, pool_w, pool_scale, conv_dw,
           conv_b, conv_ln_g, conv_ln_b, ret_gn_g, w_pool_proj, w_conv_proj, w_ret_proj, w_out, w_ffn_in,
           w_ffn_out):
    B, S, D = x.shape
    depth = w_in.shape[0]
    T = B * S
    pool_width = pool_scale.shape[-1]
    conv_width = conv_b.shape[-1]
    ret_width = ret_gn_g.shape[-1]
    n_mix = pool_width + 2 * conv_width + 4 * ret_width
    hidden = w_ffn_out.shape[1]
    head_dim = ret_width // RET_HEADS
    chunk = 128

    tm_mix = _pick(S, (256, 128))
    tm_gate = _pick(T, (2048, 1024, 512, 256, 128))
    tn_gate = _pick(N_BRANCH * D, (2048, 1024, 512, 256, 128))
    tm_merge = _pick(T, (256, 128))
    tm_ffn = _pick(T, (512, 256, 128))
    th_ffn = _pick(hidden, (512, 256, 128))

    xf = x.reshape(T, D)
    xn, cosf, sins, w_mix, w_gate = _prologue(xf, g_mix_pre[0].reshape(1, D), positions, head_dim, w_in, n_mix)
    layer_weights = (w_ffn_in, w_ffn_out, w_pool_proj, w_conv_proj, w_ret_proj, w_out)
    for l in range(depth):
        br, wfi16, wfo16, wp16, wc16, wr16, wo16 = _mixers(
            xn, w_mix, l, cosf, sins, pool_w, pool_scale[l], conv_dw[l], conv_b[l], conv_ln_g[l], conv_ln_b[l],
            ret_gn_g[l], layer_weights, seq_len=S, tm=tm_mix, chunk=chunk)
        gates = _proj_gate(xn, w_gate, tm=tm_gate, tn=tn_gate)
        xf, xn, w_mix, w_gate = _merge(br, gates, xf, wp16, wc16, wr16, wo16, l, g_mix_post[l].reshape(1, D),
                                       g_ffn_pre[l].reshape(1, D), w_in, n_mix, tm=tm_merge)
        g_next = g_mix_pre[l + 1].reshape(1, D) if l + 1 < depth else None
        xf, xn = _ffn(xn, xf, wfi16, wfo16, g_ffn_post[l].reshape(1, D), g_next, tm=tm_ffn, th=th_ffn)
    return xf.reshape(B, S, D)
```

```python
import functools
import math

import jax
import jax.numpy as jnp
from jax import lax
from jax.experimental import pallas as pl
from jax.experimental.pallas import tpu as pltpu

F32 = jnp.float32
BF16 = jnp.bfloat16

NORM_EPS = 1e-6
LN_EPS = 1e-5
ROPE_BASE = 10000.0
POOL_WINDOWS = (2, 4, 8, 16)
RET_HEADS = 8
N_BRANCH = 3

LANES = 128
SUBLANES = 8
BF16_ROWS = 16
VMEM_LIMIT_BYTES = 56 * 1024 * 1024

POOL_HALO = 16
CONV_HALO = 32
CONV_ROWS = 64
RET_HEAD_GROUP = 4
GATE_ROWS = 512
FFN_ROWS = 256
MIX_ROWS = 256


def _rms_norm(x, g):
    return x * lax.rsqrt(jnp.mean(x * x, axis=-1, keepdims=True) + NORM_EPS) * g


def _sigmoid(x):
    return 0.5 * jnp.tanh(0.5 * x) + 0.5


def _compiler_params(semantics):
    return pltpu.CompilerParams(dimension_semantics=semantics, vmem_limit_bytes=VMEM_LIMIT_BYTES)


def _pick(n, candidates):
    for c in candidates:
        if n % c == 0:
            return c
    raise ValueError(f"no tile in {candidates} divides {n}")


def _rope(pos, half):
    lane = lax.broadcasted_iota(jnp.int32, (1, 2 * half), 1)
    freq = (lane % half).astype(F32)
    inv = jnp.exp(freq * (-math.log(ROPE_BASE) / half))
    ang = pos * inv
    sin = jnp.sin(ang)
    return jnp.cos(ang), jnp.where(lane < half, -sin, sin)


def _cast_specs(w, layer, n_steps, period=1):
    _, R, C = w.shape
    rows = R * period // n_steps
    assert R * period % n_steps == 0 and rows % BF16_ROWS == 0 and C % LANES == 0
    in_spec = pl.BlockSpec((None, rows, C), lambda i: (layer, i // period, 0))
    out_spec = pl.BlockSpec((rows, C), lambda i: (i // period, 0))
    return in_spec, out_spec, jax.ShapeDtypeStruct((R, C), BF16)


def _cast_period(w, n_steps):
    period = 1
    while (w.shape[1] * period) % (n_steps * BF16_ROWS):
        period *= 2
    return period


def _split_cast_specs(w, layer, n_steps, n_first):
    _, R, C = w.shape
    rows = R // n_steps
    assert R % n_steps == 0 and rows % BF16_ROWS == 0 and n_first % LANES == 0 and C % LANES == 0
    in_spec = pl.BlockSpec((None, rows, C), lambda i: (layer, i, 0))
    out_specs = [pl.BlockSpec((rows, n_first), lambda i: (i, 0)), pl.BlockSpec((rows, C - n_first), lambda i: (i, 0))]
    out_shapes = [jax.ShapeDtypeStruct((R, n_first), BF16), jax.ShapeDtypeStruct((R, C - n_first), BF16)]
    return in_spec, out_specs, out_shapes


def _split_cast(w_ref, first_ref, second_ref):
    n_first = first_ref.shape[1]
    first_ref[...] = w_ref[:, 0:n_first].astype(first_ref.dtype)
    second_ref[...] = w_ref[:, n_first:].astype(second_ref.dtype)


def _prologue_kernel(x_ref, g_ref, pos_ref, w_ref, o_ref, cos_ref, sin_ref, wmix_ref, wgate_ref):
    o_ref[...] = _rms_norm(x_ref[...], g_ref[...]).astype(o_ref.dtype)
    cos_ref[...], sin_ref[...] = _rope(pos_ref[...], pos_ref.shape[1] // 2)
    _split_cast(w_ref, wmix_ref, wgate_ref)


def _prologue(x, g, positions, head_dim, w_in, n_mix):
    T, D = x.shape
    tm = _pick(T, (512, 256, 128))
    posb = jnp.broadcast_to(positions.astype(F32).reshape(T, 1), (T, head_dim))
    w_spec, w_out_specs, w_out_shapes = _split_cast_specs(w_in, 0, T // tm, n_mix)
    rows = pl.BlockSpec((tm, D), lambda i: (i, 0))
    rope = pl.BlockSpec((tm, head_dim), lambda i: (i, 0))
    return pl.pallas_call(
        _prologue_kernel,
        grid=(T // tm,),
        in_specs=[rows, pl.BlockSpec((1, D), lambda i: (0, 0)), rope, w_spec],
        out_specs=[rows, rope, rope] + w_out_specs,
        out_shape=[jax.ShapeDtypeStruct((T, D), BF16)] + [jax.ShapeDtypeStruct((T, head_dim), F32)] * 2
        + w_out_shapes,
        compiler_params=_compiler_params(("parallel",)),
        name="prologue",
    )(x, g, posb, w_in)


def _proj_gate_kernel(x_ref, w_ref, o_ref):
    for r0 in range(0, x_ref.shape[0], GATE_ROWS):
        rows = slice(r0, min(r0 + GATE_ROWS, x_ref.shape[0]))
        acc = jnp.dot(x_ref[rows, :], w_ref[...], preferred_element_type=F32)
        o_ref[rows, :] = _sigmoid(acc).astype(o_ref.dtype)


def _proj_gate(xn, w, *, tm, tn):
    T, D = xn.shape
    N = w.shape[1]
    return pl.pallas_call(
        _proj_gate_kernel,
        grid=(T // tm, N // tn),
        in_specs=[
            pl.BlockSpec((tm, D), lambda i, j: (i, 0)),
            pl.BlockSpec((D, tn), lambda i, j: (0, j)),
        ],
        out_specs=pl.BlockSpec((tm, tn), lambda i, j: (i, j)),
        out_shape=jax.ShapeDtypeStruct((T, N), BF16),
        compiler_params=_compiler_params(("parallel", "arbitrary")),
        name="proj_gate",
    )(xn, w)


def _rotary(t, cos, sin):
    return t * cos + pltpu.roll(t, t.shape[1] // 2, axis=1) * sin


def _mixer_kernel(x_ref, w_ref, cos_ref, sin_ref, poolw_ref, pscale_ref, dw_ref, cb_ref, lng_ref, lnb_ref,
                  gng_ref, *rest, n_casts, tm, chunk, tiles_per_seq, pool_width, conv_width, conv_kernel,
                  ret_width):
    cast_in, out_ref, cast_out = rest[:n_casts], rest[n_casts], rest[n_casts + 1:2 * n_casts + 1]
    u_scr, h_scr, state_scr, decay_scr, xi_scr, zeta_scr = rest[2 * n_casts + 1:]
    for src_ref, dst_ref in zip(cast_in, cast_out):
        dst_ref[...] = src_ref[...].astype(dst_ref.dtype)

    tile = pl.program_id(0)
    s = tile % tiles_per_seq
    heads = RET_HEADS
    dh = ret_width // heads
    pool_group = pool_width // len(POOL_WINDOWS)
    log_gamma = [math.log1p(-(2.0 ** (-5.0 - h))) for h in range(heads)]

    off_a = pool_width
    off_g = off_a + conv_width
    off_q = off_g + conv_width
    off_k = off_q + ret_width
    off_v = off_k + ret_width
    off_gate = off_v + ret_width

    @pl.when(tile == 0)
    def _():
        row = lax.broadcasted_iota(jnp.int32, (chunk, chunk), 0).astype(F32)
        col = lax.broadcasted_iota(jnp.int32, (chunk, chunk), 1).astype(F32)
        rel = row - col
        for h in range(heads):
            decay_scr[h] = jnp.where(rel >= 0, jnp.exp(jnp.maximum(rel, 0.0) * log_gamma[h]), 0.0)
            xi_scr[h] = jnp.exp((row + 1.0) * log_gamma[h])
            zeta_scr[h] = jnp.exp((chunk - 1.0 - row) * log_gamma[h])

    @pl.when(s == 0)
    def _():
        u_scr[0:POOL_HALO, :] = jnp.zeros((POOL_HALO, pool_width), F32)
        h_scr[0:CONV_HALO, :] = jnp.zeros((CONV_HALO, conv_width), F32)
        state_scr[...] = jnp.zeros_like(state_scr)

    for r_off in range(0, tm, MIX_ROWS):
        sub = slice(r_off, r_off + MIX_ROWS)

        def stream(c0, width, sub=sub):
            return jnp.dot(x_ref[sub, :], w_ref[:, c0:c0 + width], preferred_element_type=F32)

        u_scr[POOL_HALO + r_off:POOL_HALO + r_off + MIX_ROWS, :] = stream(0, pool_width)
        t_idx = s * tm + r_off + lax.broadcasted_iota(jnp.int32, (MIX_ROWS, 1), 0)
        for gi, w in enumerate(POOL_WINDOWS):
            c0 = gi * pool_group
            ug = u_scr[r_off:r_off + POOL_HALO + MIX_ROWS, c0:c0 + pool_group]
            win = ug
            span = 1
            while span < w:
                win = win + pltpu.roll(win, span, axis=0)
                span *= 2
            cnt = jnp.minimum(t_idx + 1, w).astype(F32)
            p = win[POOL_HALO:, :] / cnt - ug[POOL_HALO:, :]
            y = jnp.dot(p.astype(BF16), poolw_ref[gi].astype(BF16), preferred_element_type=F32)
            out_ref[sub, c0:c0 + pool_group] = (y * pscale_ref[:, c0:c0 + pool_group]).astype(out_ref.dtype)

        h_scr[CONV_HALO + r_off:CONV_HALO + r_off + MIX_ROWS, :] = (
            stream(off_a, conv_width) * _sigmoid(stream(off_g, conv_width)))
        win_rows = CONV_ROWS + SUBLANES
        for r0 in range(r_off, r_off + MIX_ROWS, CONV_ROWS):
            base = CONV_HALO + r0 - SUBLANES
            blocks = []
            for c0 in range(0, conv_width, LANES):
                cols = slice(c0, c0 + LANES)
                acc = None
                for r in range(SUBLANES):
                    part = None
                    for a in range((conv_kernel - 1 - r) // SUBLANES + 1):
                        j = conv_kernel - 1 - (SUBLANES * a + r)
                        lo = base - SUBLANES * a
                        term = dw_ref[j:j + 1, cols] * h_scr[lo:lo + win_rows, cols]
                        part = term if part is None else part + term
                    if r:
                        part = pltpu.roll(part, r, axis=0)
                    acc = part if acc is None else acc + part
                blocks.append(acc[SUBLANES:, :] + cb_ref[:, cols])
            conv = jnp.concatenate(blocks, axis=1)
            mu = jnp.mean(conv, axis=-1, keepdims=True)
            d = conv - mu
            var = jnp.mean(d * d, axis=-1, keepdims=True)
            y = d * lax.rsqrt(var + LN_EPS) * lng_ref[...] + lnb_ref[...]
            out_ref[r0:r0 + CONV_ROWS, pool_width:pool_width + conv_width] = (
                y * _sigmoid(y)).astype(out_ref.dtype)

        out_off = pool_width + conv_width
        cosf = cos_ref[sub, :]
        sins = sin_ref[sub, :]
        group = RET_HEAD_GROUP
        for h0 in range(0, heads, group):
            width = group * dh
            q_g = stream(off_q + h0 * dh, width)
            k_g = stream(off_k + h0 * dh, width)
            v_g = stream(off_v + h0 * dh, width)
            gate_g = stream(off_gate + h0 * dh, width)
            for hh in range(group):
                h = h0 + hh
                lanes = slice(hh * dh, (hh + 1) * dh)
                gamma_c = math.exp(chunk * log_gamma[h])
                qr = _rotary(q_g[:, lanes], cosf, sins).astype(BF16)
                kr = (_rotary(k_g[:, lanes], cosf, sins) * (dh ** -0.5)).astype(BF16)
                for c in range(MIX_ROWS // chunk):
                    rows = slice(c * chunk, (c + 1) * chunk)
                    q = qr[rows]
                    k = kr[rows]
                    v = v_g[rows, lanes]
                    scores = lax.dot_general(q, k, (((1,), (1,)), ((), ())), preferred_element_type=F32)
                    scores = scores * decay_scr[h]
                    intra = jnp.dot(scores.astype(BF16), v.astype(BF16), preferred_element_type=F32)
                    state = state_scr[h]
                    inter = jnp.dot(q, state.astype(BF16), preferred_element_type=F32) * xi_scr[h]
                    kv = lax.dot_general(k, (v * zeta_scr[h]).astype(BF16), (((0,), (0,)), ((), ())),
                                         preferred_element_type=F32)
                    state_scr[h] = gamma_c * state + kv
                    o = intra + inter
                    mu = jnp.mean(o, axis=-1, keepdims=True)
                    d = o - mu
                    var = jnp.mean(d * d, axis=-1, keepdims=True)
                    o = d * lax.rsqrt(var + LN_EPS) * gng_ref[:, h * dh:(h + 1) * dh]
                    gate = gate_g[rows, lanes]
                    out_rows = slice(r_off + c * chunk, r_off + (c + 1) * chunk)
                    out_ref[out_rows, out_off + h * dh:out_off + (h + 1) * dh] = (
                        gate * _sigmoid(gate) * o).astype(out_ref.dtype)

    u_scr[0:POOL_HALO, :] = u_scr[tm:tm + POOL_HALO, :]
    h_scr[0:CONV_HALO, :] = h_scr[tm:tm + CONV_HALO, :]


def _mixers(xn, w_mix, layer, cosf, sins, pool_w, pool_scale, conv_dw, conv_b, ln_g, ln_b, gn_g, cast_weights,
            *, seq_len, tm, chunk):
    T, d_model = xn.shape
    pool_width = pool_scale.shape[-1]
    conv_kernel, conv_width = conv_dw.shape
    ret_width = gn_g.shape[-1]
    n_mix = pool_width + 2 * conv_width + 4 * ret_width
    dh = ret_width // RET_HEADS
    n_groups = len(POOL_WINDOWS)
    pool_group = pool_width // n_groups
    n_steps = T // tm
    assert pool_width + conv_width + ret_width == d_model and w_mix.shape == (d_model, n_mix)
    assert dh == LANES and pool_group == LANES and conv_width % LANES == 0 and RET_HEADS % RET_HEAD_GROUP == 0
    assert chunk % SUBLANES == 0 and MIX_ROWS % chunk == 0 and MIX_ROWS % CONV_ROWS == 0
    assert tm % MIX_ROWS == 0 and seq_len % tm == 0
    assert max(POOL_WINDOWS) - 1 <= POOL_HALO and SUBLANES * ((conv_kernel - 1) // SUBLANES + 1) <= CONV_HALO

    def row(v):
        return v.reshape(1, -1)

    def const_spec(shape):
        return pl.BlockSpec(shape, lambda i: (0,) * len(shape))

    casts = [_cast_specs(w, layer, n_steps, _cast_period(w, n_steps)) for w in cast_weights]

    kern = functools.partial(_mixer_kernel, n_casts=len(casts), tm=tm, chunk=chunk, tiles_per_seq=seq_len // tm,
                             pool_width=pool_width, conv_width=conv_width, conv_kernel=conv_kernel,
                             ret_width=ret_width)
    return pl.pallas_call(
        kern,
        grid=(n_steps,),
        in_specs=[
            pl.BlockSpec((tm, d_model), lambda i: (i, 0)),
            pl.BlockSpec((d_model, n_mix), lambda i: (0, 0), pipeline_mode=pl.Buffered(1)),
            pl.BlockSpec((tm, dh), lambda i: (i, 0)),
            pl.BlockSpec((tm, dh), lambda i: (i, 0)),
            pl.BlockSpec((None, n_groups, pool_group, pool_group), lambda i: (layer, 0, 0, 0)),
            const_spec((1, pool_width)),
            const_spec((conv_kernel, conv_width)),
            const_spec((1, conv_width)),
            const_spec((1, conv_width)),
            const_spec((1, conv_width)),
            const_spec((1, ret_width)),
        ] + [c[0] for c in casts],
        out_specs=[pl.BlockSpec((tm, d_model), lambda i: (i, 0))] + [c[1] for c in casts],
        out_shape=[jax.ShapeDtypeStruct((T, d_model), BF16)] + [c[2] for c in casts],
        scratch_shapes=[
            pltpu.VMEM((POOL_HALO + tm, pool_width), F32),
            pltpu.VMEM((CONV_HALO + tm, conv_width), F32),
            pltpu.VMEM((RET_HEADS, dh, dh), F32),
            pltpu.VMEM((RET_HEADS, chunk, chunk), F32),
            pltpu.VMEM((RET_HEADS, chunk, dh), F32),
            pltpu.VMEM((RET_HEADS, chunk, dh), F32),
        ],
        compiler_params=_compiler_params(("arbitrary",)),
        name="mixers",
    )(xn, w_mix, cosf, sins, pool_w, row(pool_scale), conv_dw, row(conv_b), row(ln_g), row(ln_b), row(gn_g),
      *cast_weights)


def _merge_kernel(br_ref, gate_ref, x_ref, wp_ref, wc_ref, wr_ref, wo_ref, g_ref, gn_ref, *rest,
                  pool_width, conv_width, cast_next, n_casts):
    n_in = n_casts + (1 if cast_next else 0)
    ins, outs = rest[:n_in], rest[n_in:]
    o_ref, xn_ref = outs[0], outs[1]
    if cast_next:
        _split_cast(ins[0], outs[2], outs[3])
    for src_ref, dst_ref in zip(ins[n_in - n_casts:], outs[len(outs) - n_casts:]):
        dst_ref[...] = src_ref[...].astype(dst_ref.dtype)
    d = x_ref.shape[-1]
    c1 = pool_width
    c2 = pool_width + conv_width
    y_pool = jnp.dot(br_ref[:, 0:c1], wp_ref[...], preferred_element_type=F32)
    merged = gate_ref[:, 0:d].astype(F32) * y_pool
    y_conv = jnp.dot(br_ref[:, c1:c2], wc_ref[...], preferred_element_type=F32)
    merged = merged + gate_ref[:, d:2 * d].astype(F32) * y_conv
    y_ret = jnp.dot(br_ref[:, c2:d], wr_ref[...], preferred_element_type=F32)
    merged = merged + gate_ref[:, 2 * d:3 * d].astype(F32) * y_ret
    out = jnp.dot(merged.astype(BF16), wo_ref[...], preferred_element_type=F32)
    x_new = x_ref[...] + _rms_norm(out, g_ref[...])
    o_ref[...] = x_new
    xn_ref[...] = _rms_norm(x_new, gn_ref[...]).astype(xn_ref.dtype)


def _merge(br, gates, x, wp, wc, wr, wo, layer, g_post, g_next, w_in, n_mix, cast_weights, *, tm):
    T, D = x.shape
    pool_width, conv_width, ret_width = wp.shape[0], wc.shape[0], wr.shape[0]
    cast_next = layer + 1 < w_in.shape[0]

    def resident(rows):
        return pl.BlockSpec((rows, D), lambda i: (0, 0), pipeline_mode=pl.Buffered(1))

    def vec():
        return pl.BlockSpec((1, D), lambda i: (0, 0))

    def rows(width):
        return pl.BlockSpec((tm, width), lambda i: (i, 0))

    in_specs = [rows(D), rows(N_BRANCH * D), rows(D), resident(pool_width), resident(conv_width),
                resident(ret_width), resident(D), vec(), vec()]
    args = [br, gates, x, wp, wc, wr, wo, g_post, g_next]
    out_specs = [rows(D), rows(D)]
    out_shape = [jax.ShapeDtypeStruct((T, D), F32), jax.ShapeDtypeStruct((T, D), BF16)]
    if cast_next:
        w_spec, w_out_specs, w_out_shapes = _split_cast_specs(w_in, layer + 1, T // tm, n_mix)
        in_specs.append(w_spec)
        args.append(w_in)
        out_specs += w_out_specs
        out_shape += w_out_shapes
    for w in cast_weights:
        c_in, c_out, c_shape = _cast_specs(w, layer, T // tm, _cast_period(w, T // tm))
        in_specs.append(c_in)
        args.append(w)
        out_specs.append(c_out)
        out_shape.append(c_shape)
    outs = pl.pallas_call(
        functools.partial(_merge_kernel, pool_width=pool_width, conv_width=conv_width, cast_next=cast_next,
                          n_casts=len(cast_weights)),
        grid=(T // tm,),
        in_specs=in_specs,
        out_specs=out_specs,
        out_shape=out_shape,
        compiler_params=_compiler_params(("parallel",)),
        name="merge",
    )(*args)
    next_w = tuple(outs[2:4]) if cast_next else (None, None)
    return (outs[0], outs[1]) + next_w + tuple(outs[len(outs) - len(cast_weights):])


def _ffn_kernel(xn_ref, x_ref, wa_ref, wb_ref, wo_ref, gpost_ref, *rest, emit_next):
    if emit_next:
        gnext_ref, o_ref, xn_out_ref = rest
    else:
        (o_ref,) = rest
    j = pl.program_id(1)
    last = pl.num_programs(1) - 1

    def step(first, final):
        for r0 in range(0, xn_ref.shape[0], FFN_ROWS):
            rows = slice(r0, min(r0 + FFN_ROWS, xn_ref.shape[0]))
            xn = xn_ref[rows, :]
            a = jnp.dot(xn, wa_ref[...], preferred_element_type=F32)
            b = jnp.dot(xn, wb_ref[...], preferred_element_type=F32)
            hid = (a * _sigmoid(a) * b).astype(BF16)
            acc = jnp.dot(hid, wo_ref[...], preferred_element_type=F32)
            if not first:
                acc = acc + o_ref[rows, :]
            if final:
                x_new = x_ref[rows, :] + _rms_norm(acc, gpost_ref[...])
                o_ref[rows, :] = x_new
                if emit_next:
                    xn_out_ref[rows, :] = _rms_norm(x_new, gnext_ref[...]).astype(xn_out_ref.dtype)
            else:
                o_ref[rows, :] = acc

    pl.when(j == 0)(functools.partial(step, True, False))
    pl.when((j > 0) & (j < last))(functools.partial(step, False, False))
    pl.when(j == last)(functools.partial(step, False, True))


def _ffn(xn, x, w_in, w_out, g_post, g_next, *, tm, th):
    T, D = x.shape
    hidden = w_out.shape[0]
    nh = hidden // th
    emit_next = g_next is not None

    def rows():
        return pl.BlockSpec((tm, D), lambda i, j: (i, 0))

    def vec():
        return pl.BlockSpec((1, D), lambda i, j: (0, 0))

    in_specs = [
        rows(), rows(),
        pl.BlockSpec((D, th), lambda i, j: (0, j)),
        pl.BlockSpec((D, th), lambda i, j: (0, j + nh)),
        pl.BlockSpec((th, D), lambda i, j: (j, 0)),
        vec(),
    ]
    args = [xn, x, w_in, w_in, w_out, g_post]
    out_specs = [rows()]
    out_shape = [jax.ShapeDtypeStruct((T, D), F32)]
    if emit_next:
        in_specs.append(vec())
        args.append(g_next)
        out_specs.append(rows())
        out_shape.append(jax.ShapeDtypeStruct((T, D), BF16))
    outs = pl.pallas_call(
        functools.partial(_ffn_kernel, emit_next=emit_next),
        grid=(T // tm, nh),
        in_specs=in_specs,
        out_specs=out_specs,
        out_shape=out_shape,
        compiler_params=_compiler_params(("parallel", "arbitrary")),
        name="ffn",
    )(*args)
    return (outs[0], outs[1]) if emit_next else (outs[0], None)


def kernel(x, positions, g_mix_pre, g_mix_post, g_ffn_pre, g_ffn_post, w_in, pool_w, pool_scale, conv_dw,
           conv_b, conv_ln_g, conv_ln_b, ret_gn_g, w_pool_proj, w_conv_proj, w_ret_proj, w_out, w_ffn_in,
           w_ffn_out):
    B, S, D = x.shape
    depth = w_in.shape[0]
    T = B * S
    pool_width = pool_scale.shape[-1]
    conv_width = conv_b.shape[-1]
    ret_width = ret_gn_g.shape[-1]
    n_mix = pool_width + 2 * conv_width + 4 * ret_width
    hidden = w_ffn_out.shape[1]
    head_dim = ret_width // RET_HEADS
    chunk = 128

    tm_mix = _pick(S, (2 * MIX_ROWS, MIX_ROWS))
    tm_gate = _pick(T, (2048, 1024, 512, 256, 128))
    tn_gate = _pick(N_BRANCH * D, (2048, 1024, 512, 256, 128))
    tm_merge = _pick(T, (256, 128))
    tm_ffn = _pick(T, (512, 256, 128))
    th_ffn = _pick(hidden, (512, 256, 128))

    xf = x.reshape(T, D)
    xn, cosf, sins, w_mix, w_gate = _prologue(xf, g_mix_pre[0].reshape(1, D), positions, head_dim, w_in, n_mix)
    merge_weights = (w_pool_proj, w_conv_proj, w_ret_proj, w_out)
    ffn_weights = (w_ffn_in, w_ffn_out)
    for l in range(depth):
        br, wp16, wc16, wr16, wo16 = _mixers(
            xn, w_mix, l, cosf, sins, pool_w, pool_scale[l], conv_dw[l], conv_b[l], conv_ln_g[l], conv_ln_b[l],
            ret_gn_g[l], merge_weights, seq_len=S, tm=tm_mix, chunk=chunk)
        gates = _proj_gate(xn, w_gate, tm=tm_gate, tn=tn_gate)
        xf, xn, w_mix, w_gate, wfi16, wfo16 = _merge(
            br, gates, xf, wp16, wc16, wr16, wo16, l, g_mix_post[l].reshape(1, D), g_ffn_pre[l].reshape(1, D),
            w_in, n_mix, ffn_weights, tm=tm_merge)
        g_next = g_mix_pre[l + 1].reshape(1, D) if l + 1 < depth else None
        xf, xn = _ffn(xn, xf, wfi16, wfo16, g_ffn_post[l].reshape(1, D), g_next, tm=tm_ffn, th=th_ffn)
    return xf.reshape(B, S, D)
```

```python
import functools
import math

import jax
import jax.numpy as jnp
from jax import lax
from jax.experimental import pallas as pl
from jax.experimental.pallas import tpu as pltpu

F32 = jnp.float32
BF16 = jnp.bfloat16

NORM_EPS = 1e-6
LN_EPS = 1e-5
ROPE_BASE = 10000.0
POOL_WINDOWS = (2, 4, 8, 16)
RET_HEADS = 8
N_BRANCH = 3

LANES = 128
SUBLANES = 8
BF16_ROWS = 16
VMEM_LIMIT_BYTES = 56 * 1024 * 1024

POOL_HALO = 16
CONV_HALO = 32
CONV_ROWS = 64
RET_HEAD_GROUP = 4
GATE_ROWS = 512
MIX_ROWS = 256


def _rms_norm(x, g):
    return x * lax.rsqrt(jnp.mean(x * x, axis=-1, keepdims=True) + NORM_EPS) * g


def _sigmoid(x):
    return 0.5 * jnp.tanh(0.5 * x) + 0.5


def _compiler_params(semantics):
    return pltpu.CompilerParams(dimension_semantics=semantics, vmem_limit_bytes=VMEM_LIMIT_BYTES)


def _pick(n, candidates):
    for c in candidates:
        if n % c == 0:
            return c
    raise ValueError(f"no tile in {candidates} divides {n}")


def _rope(pos, half):
    lane = lax.broadcasted_iota(jnp.int32, (1, 2 * half), 1)
    freq = (lane % half).astype(F32)
    inv = jnp.exp(freq * (-math.log(ROPE_BASE) / half))
    ang = pos * inv
    sin = jnp.sin(ang)
    return jnp.cos(ang), jnp.where(lane < half, -sin, sin)


def _cast_specs(w, layer, n_steps, period=1):
    _, R, C = w.shape
    rows = R * period // n_steps
    assert R * period % n_steps == 0 and rows % BF16_ROWS == 0 and C % LANES == 0
    in_spec = pl.BlockSpec((None, rows, C), lambda i: (layer, i // period, 0))
    out_spec = pl.BlockSpec((rows, C), lambda i: (i // period, 0))
    return in_spec, out_spec, jax.ShapeDtypeStruct((R, C), BF16)


def _cast_period(w, n_steps):
    period = 1
    while (w.shape[1] * period) % (n_steps * BF16_ROWS):
        period *= 2
    return period


def _split_cast_specs(w, layer, n_steps, n_first):
    _, R, C = w.shape
    rows = R // n_steps
    assert R % n_steps == 0 and rows % BF16_ROWS == 0 and n_first % LANES == 0 and C % LANES == 0
    in_spec = pl.BlockSpec((None, rows, C), lambda i: (layer, i, 0))
    out_specs = [pl.BlockSpec((rows, n_first), lambda i: (i, 0)), pl.BlockSpec((rows, C - n_first), lambda i: (i, 0))]
    out_shapes = [jax.ShapeDtypeStruct((R, n_first), BF16), jax.ShapeDtypeStruct((R, C - n_first), BF16)]
    return in_spec, out_specs, out_shapes


def _split_cast(w_ref, first_ref, second_ref):
    n_first = first_ref.shape[1]
    first_ref[...] = w_ref[:, 0:n_first].astype(first_ref.dtype)
    second_ref[...] = w_ref[:, n_first:].astype(second_ref.dtype)


def _prologue_kernel(x_ref, g_ref, pos_ref, w_ref, o_ref, cos_ref, sin_ref, wmix_ref, wgate_ref):
    o_ref[...] = _rms_norm(x_ref[...], g_ref[...]).astype(o_ref.dtype)
    cos_ref[...], sin_ref[...] = _rope(pos_ref[...], pos_ref.shape[1] // 2)
    _split_cast(w_ref, wmix_ref, wgate_ref)


def _prologue(x, g, positions, head_dim, w_in, n_mix):
    T, D = x.shape
    tm = _pick(T, (512, 256, 128))
    posb = jnp.broadcast_to(positions.astype(F32).reshape(T, 1), (T, head_dim))
    w_spec, w_out_specs, w_out_shapes = _split_cast_specs(w_in, 0, T // tm, n_mix)
    rows = pl.BlockSpec((tm, D), lambda i: (i, 0))
    rope = pl.BlockSpec((tm, head_dim), lambda i: (i, 0))
    return pl.pallas_call(
        _prologue_kernel,
        grid=(T // tm,),
        in_specs=[rows, pl.BlockSpec((1, D), lambda i: (0, 0)), rope, w_spec],
        out_specs=[rows, rope, rope] + w_out_specs,
        out_shape=[jax.ShapeDtypeStruct((T, D), BF16)] + [jax.ShapeDtypeStruct((T, head_dim), F32)] * 2
        + w_out_shapes,
        compiler_params=_compiler_params(("parallel",)),
        name="prologue",
    )(x, g, posb, w_in)


def _proj_gate_kernel(x_ref, w_ref, o_ref):
    for r0 in range(0, x_ref.shape[0], GATE_ROWS):
        rows = slice(r0, min(r0 + GATE_ROWS, x_ref.shape[0]))
        acc = jnp.dot(x_ref[rows, :], w_ref[...], preferred_element_type=F32)
        o_ref[rows, :] = _sigmoid(acc).astype(o_ref.dtype)


def _proj_gate(xn, w, *, tm, tn):
    T, D = xn.shape
    N = w.shape[1]
    return pl.pallas_call(
        _proj_gate_kernel,
        grid=(T // tm, N // tn),
        in_specs=[
            pl.BlockSpec((tm, D), lambda i, j: (i, 0)),
            pl.BlockSpec((D, tn), lambda i, j: (0, j)),
        ],
        out_specs=pl.BlockSpec((tm, tn), lambda i, j: (i, j)),
        out_shape=jax.ShapeDtypeStruct((T, N), BF16),
        compiler_params=_compiler_params(("parallel", "arbitrary")),
        name="proj_gate",
    )(xn, w)


def _rotary(t, cos, sin):
    return t * cos + pltpu.roll(t, t.shape[1] // 2, axis=1) * sin


def _mixer_kernel(x_ref, w_ref, cos_ref, sin_ref, poolw_ref, pscale_ref, dw_ref, cb_ref, lng_ref, lnb_ref,
                  gng_ref, *rest, n_casts, tm, chunk, tiles_per_seq, pool_width, conv_width, conv_kernel,
                  ret_width):
    cast_in, out_ref, cast_out = rest[:n_casts], rest[n_casts], rest[n_casts + 1:2 * n_casts + 1]
    u_scr, h_scr, state_scr, decay_scr, xi_scr, zeta_scr = rest[2 * n_casts + 1:]
    for src_ref, dst_ref in zip(cast_in, cast_out):
        dst_ref[...] = src_ref[...].astype(dst_ref.dtype)

    tile = pl.program_id(0)
    s = tile % tiles_per_seq
    heads = RET_HEADS
    dh = ret_width // heads
    pool_group = pool_width // len(POOL_WINDOWS)
    log_gamma = [math.log1p(-(2.0 ** (-5.0 - h))) for h in range(heads)]

    off_a = pool_width
    off_g = off_a + conv_width
    off_q = off_g + conv_width
    off_k = off_q + ret_width
    off_v = off_k + ret_width
    off_gate = off_v + ret_width

    @pl.when(tile == 0)
    def _():
        row = lax.broadcasted_iota(jnp.int32, (chunk, chunk), 0).astype(F32)
        col = lax.broadcasted_iota(jnp.int32, (chunk, chunk), 1).astype(F32)
        rel = row - col
        for h in range(heads):
            decay_scr[h] = jnp.where(rel >= 0, jnp.exp(jnp.maximum(rel, 0.0) * log_gamma[h]), 0.0)
            xi_scr[h] = jnp.exp((row + 1.0) * log_gamma[h])
            zeta_scr[h] = jnp.exp((chunk - 1.0 - row) * log_gamma[h])

    @pl.when(s == 0)
    def _():
        u_scr[0:POOL_HALO, :] = jnp.zeros((POOL_HALO, pool_width), F32)
        h_scr[0:CONV_HALO, :] = jnp.zeros((CONV_HALO, conv_width), F32)
        state_scr[...] = jnp.zeros_like(state_scr)

    for r_off in range(0, tm, MIX_ROWS):
        sub = slice(r_off, r_off + MIX_ROWS)

        def stream(c0, width, sub=sub):
            return jnp.dot(x_ref[sub, :], w_ref[:, c0:c0 + width], preferred_element_type=F32)

        u_scr[POOL_HALO + r_off:POOL_HALO + r_off + MIX_ROWS, :] = stream(0, pool_width)
        t_idx = s * tm + r_off + lax.broadcasted_iota(jnp.int32, (MIX_ROWS, 1), 0)
        for gi, w in enumerate(POOL_WINDOWS):
            c0 = gi * pool_group
            ug = u_scr[r_off:r_off + POOL_HALO + MIX_ROWS, c0:c0 + pool_group]
            win = ug
            span = 1
            while span < w:
                win = win + pltpu.roll(win, span, axis=0)
                span *= 2
            cnt = jnp.minimum(t_idx + 1, w).astype(F32)
            p = win[POOL_HALO:, :] / cnt - ug[POOL_HALO:, :]
            y = jnp.dot(p.astype(BF16), poolw_ref[gi].astype(BF16), preferred_element_type=F32)
            out_ref[sub, c0:c0 + pool_group] = (y * pscale_ref[:, c0:c0 + pool_group]).astype(out_ref.dtype)

        h_scr[CONV_HALO + r_off:CONV_HALO + r_off + MIX_ROWS, :] = (
            stream(off_a, conv_width) * _sigmoid(stream(off_g, conv_width)))
        win_rows = CONV_ROWS + SUBLANES
        for r0 in range(r_off, r_off + MIX_ROWS, CONV_ROWS):
            base = CONV_HALO + r0 - SUBLANES
            blocks = []
            for c0 in range(0, conv_width, LANES):
                cols = slice(c0, c0 + LANES)
                acc = None
                for r in range(SUBLANES):
                    part = None
                    for a in range((conv_kernel - 1 - r) // SUBLANES + 1):
                        j = conv_kernel - 1 - (SUBLANES * a + r)
                        lo = base - SUBLANES * a
                        term = dw_ref[j:j + 1, cols] * h_scr[lo:lo + win_rows, cols]
                        part = term if part is None else part + term
                    if r:
                        part = pltpu.roll(part, r, axis=0)
                    acc = part if acc is None else acc + part
                blocks.append(acc[SUBLANES:, :] + cb_ref[:, cols])
            conv = jnp.concatenate(blocks, axis=1)
            mu = jnp.mean(conv, axis=-1, keepdims=True)
            d = conv - mu
            var = jnp.mean(d * d, axis=-1, keepdims=True)
            y = d * lax.rsqrt(var + LN_EPS) * lng_ref[...] + lnb_ref[...]
            out_ref[r0:r0 + CONV_ROWS, pool_width:pool_width + conv_width] = (
                y * _sigmoid(y)).astype(out_ref.dtype)

        out_off = pool_width + conv_width
        cosf = cos_ref[sub, :]
        sins = sin_ref[sub, :]
        group = RET_HEAD_GROUP
        for h0 in range(0, heads, group):
            width = group * dh
            q_g = stream(off_q + h0 * dh, width)
            k_g = stream(off_k + h0 * dh, width)
            v_g = stream(off_v + h0 * dh, width)
            gate_g = stream(off_gate + h0 * dh, width)
            for hh in range(group):
                h = h0 + hh
                lanes = slice(hh * dh, (hh + 1) * dh)
                gamma_c = math.exp(chunk * log_gamma[h])
                qr = _rotary(q_g[:, lanes], cosf, sins).astype(BF16)
                kr = (_rotary(k_g[:, lanes], cosf, sins) * (dh ** -0.5)).astype(BF16)
                for c in range(MIX_ROWS // chunk):
                    rows = slice(c * chunk, (c + 1) * chunk)
                    q = qr[rows]
                    k = kr[rows]
                    v = v_g[rows, lanes]
                    scores = lax.dot_general(q, k, (((1,), (1,)), ((), ())), preferred_element_type=F32)
                    scores = scores * decay_scr[h]
                    intra = jnp.dot(scores.astype(BF16), v.astype(BF16), preferred_element_type=F32)
                    state = state_scr[h]
                    inter = jnp.dot(q, state.astype(BF16), preferred_element_type=F32) * xi_scr[h]
                    kv = lax.dot_general(k, (v * zeta_scr[h]).astype(BF16), (((0,), (0,)), ((), ())),
                                         preferred_element_type=F32)
                    state_scr[h] = gamma_c * state + kv
                    o = intra + inter
                    mu = jnp.mean(o, axis=-1, keepdims=True)
                    d = o - mu
                    var = jnp.mean(d * d, axis=-1, keepdims=True)
                    o = d * lax.rsqrt(var + LN_EPS) * gng_ref[:, h * dh:(h + 1) * dh]
                    gate = gate_g[rows, lanes]
                    out_rows = slice(r_off + c * chunk, r_off + (c + 1) * chunk)
                    out_ref[out_rows, out_off + h * dh:out_off + (h + 1) * dh] = (
                        gate * _sigmoid(gate) * o).astype(out_ref.dtype)

    u_scr[0:POOL_HALO, :] = u_scr[tm:tm + POOL_HALO, :]
    h_scr[0:CONV_HALO, :] = h_scr[tm:tm + CONV_HALO, :]


def _mixers(xn, w_mix, layer, cosf, sins, pool_w, pool_scale, conv_dw, conv_b, ln_g, ln_b, gn_g, cast_weights,
            *, seq_len, tm, chunk):
    T, d_model = xn.shape
    pool_width = pool_scale.shape[-1]
    conv_kernel, conv_width = conv_dw.shape
    ret_width = gn_g.shape[-1]
    n_mix = pool_width + 2 * conv_width + 4 * ret_width
    dh = ret_width // RET_HEADS
    n_groups = len(POOL_WINDOWS)
    pool_group = pool_width // n_groups
    n_steps = T // tm
    assert pool_width + conv_width + ret_width == d_model and w_mix.shape == (d_model, n_mix)
    assert dh == LANES and pool_group == LANES and conv_width % LANES == 0 and RET_HEADS % RET_HEAD_GROUP == 0
    assert chunk % SUBLANES == 0 and MIX_ROWS % chunk == 0 and MIX_ROWS % CONV_ROWS == 0
    assert tm % MIX_ROWS == 0 and seq_len % tm == 0
    assert max(POOL_WINDOWS) - 1 <= POOL_HALO and SUBLANES * ((conv_kernel - 1) // SUBLANES + 1) <= CONV_HALO

    def row(v):
        return v.reshape(1, -1)

    def const_spec(shape):
        return pl.BlockSpec(shape, lambda i: (0,) * len(shape))

    casts = [_cast_specs(w, layer, n_steps, _cast_period(w, n_steps)) for w in cast_weights]

    kern = functools.partial(_mixer_kernel, n_casts=len(casts), tm=tm, chunk=chunk, tiles_per_seq=seq_len // tm,
                             pool_width=pool_width, conv_width=conv_width, conv_kernel=conv_kernel,
                             ret_width=ret_width)
    return pl.pallas_call(
        kern,
        grid=(n_steps,),
        in_specs=[
            pl.BlockSpec((tm, d_model), lambda i: (i, 0)),
            pl.BlockSpec((d_model, n_mix), lambda i: (0, 0), pipeline_mode=pl.Buffered(1)),
            pl.BlockSpec((tm, dh), lambda i: (i, 0)),
            pl.BlockSpec((tm, dh), lambda i: (i, 0)),
            pl.BlockSpec((None, n_groups, pool_group, pool_group), lambda i: (layer, 0, 0, 0)),
            const_spec((1, pool_width)),
            const_spec((conv_kernel, conv_width)),
            const_spec((1, conv_width)),
            const_spec((1, conv_width)),
            const_spec((1, conv_width)),
            const_spec((1, ret_width)),
        ] + [c[0] for c in casts],
        out_specs=[pl.BlockSpec((tm, d_model), lambda i: (i, 0))] + [c[1] for c in casts],
        out_shape=[jax.ShapeDtypeStruct((T, d_model), BF16)] + [c[2] for c in casts],
        scratch_shapes=[
            pltpu.VMEM((POOL_HALO + tm, pool_width), F32),
            pltpu.VMEM((CONV_HALO + tm, conv_width), F32),
            pltpu.VMEM((RET_HEADS, dh, dh), F32),
            pltpu.VMEM((RET_HEADS, chunk, chunk), F32),
            pltpu.VMEM((RET_HEADS, chunk, dh), F32),
            pltpu.VMEM((RET_HEADS, chunk, dh), F32),
        ],
        compiler_params=_compiler_params(("arbitrary",)),
        name="mixers",
    )(xn, w_mix, cosf, sins, pool_w, row(pool_scale), conv_dw, row(conv_b), row(ln_g), row(ln_b), row(gn_g),
      *cast_weights)


def _merge_kernel(br_ref, gate_ref, x_ref, wp_ref, wc_ref, wr_ref, wo_ref, g_ref, gn_ref, *rest,
                  pool_width, conv_width, cast_next, n_casts):
    n_in = n_casts + (1 if cast_next else 0)
    ins, outs = rest[:n_in], rest[n_in:]
    o_ref, xn_ref = outs[0], outs[1]
    if cast_next:
        _split_cast(ins[0], outs[2], outs[3])
    for src_ref, dst_ref in zip(ins[n_in - n_casts:], outs[len(outs) - n_casts:]):
        dst_ref[...] = src_ref[...].astype(dst_ref.dtype)
    d = x_ref.shape[-1]
    c1 = pool_width
    c2 = pool_width + conv_width
    y_pool = jnp.dot(br_ref[:, 0:c1], wp_ref[...], preferred_element_type=F32)
    merged = gate_ref[:, 0:d].astype(F32) * y_pool
    y_conv = jnp.dot(br_ref[:, c1:c2], wc_ref[...], preferred_element_type=F32)
    merged = merged + gate_ref[:, d:2 * d].astype(F32) * y_conv
    y_ret = jnp.dot(br_ref[:, c2:d], wr_ref[...], preferred_element_type=F32)
    merged = merged + gate_ref[:, 2 * d:3 * d].astype(F32) * y_ret
    out = jnp.dot(merged.astype(BF16), wo_ref[...], preferred_element_type=F32)
    x_new = x_ref[...] + _rms_norm(out, g_ref[...])
    o_ref[...] = x_new
    xn_ref[...] = _rms_norm(x_new, gn_ref[...]).astype(xn_ref.dtype)


def _merge(br, gates, x, wp, wc, wr, wo, layer, g_post, g_next, w_in, n_mix, cast_weights, *, tm):
    T, D = x.shape
    pool_width, conv_width, ret_width = wp.shape[0], wc.shape[0], wr.shape[0]
    cast_next = layer + 1 < w_in.shape[0]

    def resident(rows):
        return pl.BlockSpec((rows, D), lambda i: (0, 0), pipeline_mode=pl.Buffered(1))

    def vec():
        return pl.BlockSpec((1, D), lambda i: (0, 0))

    def rows(width):
        return pl.BlockSpec((tm, width), lambda i: (i, 0))

    in_specs = [rows(D), rows(N_BRANCH * D), rows(D), resident(pool_width), resident(conv_width),
                resident(ret_width), resident(D), vec(), vec()]
    args = [br, gates, x, wp, wc, wr, wo, g_post, g_next]
    out_specs = [rows(D), rows(D)]
    out_shape = [jax.ShapeDtypeStruct((T, D), F32), jax.ShapeDtypeStruct((T, D), BF16)]
    if cast_next:
        w_spec, w_out_specs, w_out_shapes = _split_cast_specs(w_in, layer + 1, T // tm, n_mix)
        in_specs.append(w_spec)
        args.append(w_in)
        out_specs += w_out_specs
        out_shape += w_out_shapes
    for w in cast_weights:
        c_in, c_out, c_shape = _cast_specs(w, layer, T // tm, _cast_period(w, T // tm))
        in_specs.append(c_in)
        args.append(w)
        out_specs.append(c_out)
        out_shape.append(c_shape)
    outs = pl.pallas_call(
        functools.partial(_merge_kernel, pool_width=pool_width, conv_width=conv_width, cast_next=cast_next,
                          n_casts=len(cast_weights)),
        grid=(T // tm,),
        in_specs=in_specs,
        out_specs=out_specs,
        out_shape=out_shape,
        compiler_params=_compiler_params(("parallel",)),
        name="merge",
    )(*args)
    next_w = tuple(outs[2:4]) if cast_next else (None, None)
    return (outs[0], outs[1]) + next_w + tuple(outs[len(outs) - len(cast_weights):])


def _ffn_up_kernel(x_ref, wa_ref, wb_ref, o_ref):
    for r0 in range(0, x_ref.shape[0], GATE_ROWS):
        rows = slice(r0, min(r0 + GATE_ROWS, x_ref.shape[0]))
        xn = x_ref[rows, :]
        a = jnp.dot(xn, wa_ref[...], preferred_element_type=F32)
        b = jnp.dot(xn, wb_ref[...], preferred_element_type=F32)
        o_ref[rows, :] = (a * _sigmoid(a) * b).astype(o_ref.dtype)


def _ffn_up(xn, w_in, *, tm, tn):
    T, D = xn.shape
    hidden = w_in.shape[1] // 2
    nh = hidden // tn
    return pl.pallas_call(
        _ffn_up_kernel,
        grid=(T // tm, nh),
        in_specs=[
            pl.BlockSpec((tm, D), lambda i, j: (i, 0)),
            pl.BlockSpec((D, tn), lambda i, j: (0, j)),
            pl.BlockSpec((D, tn), lambda i, j: (0, j + nh)),
        ],
        out_specs=pl.BlockSpec((tm, tn), lambda i, j: (i, j)),
        out_shape=jax.ShapeDtypeStruct((T, hidden), BF16),
        compiler_params=_compiler_params(("parallel", "arbitrary")),
        name="ffn_up",
    )(xn, w_in, w_in)


def _ffn_down_kernel(h_ref, x_ref, wo_ref, gpost_ref, *rest, emit_next):
    if emit_next:
        gnext_ref, o_ref, xn_out_ref = rest
    else:
        (o_ref,) = rest
    k = pl.program_id(1)

    @pl.when(k == 0)
    def _():
        o_ref[...] = jnp.zeros_like(o_ref)

    o_ref[...] += jnp.dot(h_ref[...], wo_ref[...], preferred_element_type=F32)

    @pl.when(k == pl.num_programs(1) - 1)
    def _():
        x_new = x_ref[...] + _rms_norm(o_ref[...], gpost_ref[...])
        o_ref[...] = x_new
        if emit_next:
            xn_out_ref[...] = _rms_norm(x_new, gnext_ref[...]).astype(xn_out_ref.dtype)


def _ffn_down(hid, x, w_out, g_post, g_next, *, tm, tk):
    T, D = x.shape
    hidden = w_out.shape[0]
    emit_next = g_next is not None

    def rows():
        return pl.BlockSpec((tm, D), lambda i, k: (i, 0))

    def vec():
        return pl.BlockSpec((1, D), lambda i, k: (0, 0))

    in_specs = [pl.BlockSpec((tm, tk), lambda i, k: (i, k)), rows(), pl.BlockSpec((tk, D), lambda i, k: (k, 0)), vec()]
    args = [hid, x, w_out, g_post]
    out_specs = [rows()]
    out_shape = [jax.ShapeDtypeStruct((T, D), F32)]
    if emit_next:
        in_specs.append(vec())
        args.append(g_next)
        out_specs.append(rows())
        out_shape.append(jax.ShapeDtypeStruct((T, D), BF16))
    outs = pl.pallas_call(
        functools.partial(_ffn_down_kernel, emit_next=emit_next),
        grid=(T // tm, hidden // tk),
        in_specs=in_specs,
        out_specs=out_specs,
        out_shape=out_shape,
        compiler_params=_compiler_params(("parallel", "arbitrary")),
        name="ffn_down",
    )(*args)
    return (outs[0], outs[1]) if emit_next else (outs[0], None)


def kernel(x, positions, g_mix_pre, g_mix_post, g_ffn_pre, g_ffn_post, w_in, pool_w, pool_scale, conv_dw,
           conv_b, conv_ln_g, conv_ln_b, ret_gn_g, w_pool_proj, w_conv_proj, w_ret_proj, w_out, w_ffn_in,
           w_ffn_out):
    B, S, D = x.shape
    depth = w_in.shape[0]
    T = B * S
    pool_width = pool_scale.shape[-1]
    conv_width = conv_b.shape[-1]
    ret_width = ret_gn_g.shape[-1]
    n_mix = pool_width + 2 * conv_width + 4 * ret_width
    hidden = w_ffn_out.shape[1]
    head_dim = ret_width // RET_HEADS
    chunk = 128

    tm_mix = _pick(S, (MIX_ROWS,))
    tm_gate = _pick(T, (2048, 1024, 512, 256, 128))
    tn_gate = _pick(N_BRANCH * D, (2048, 1024, 512, 256, 128))
    tm_merge = _pick(T, (256, 128))
    tm_up = _pick(T, (2048, 1024, 512, 256, 128))
    tn_up = _pick(hidden, (512, 256, 128))
    tm_down = _pick(T, (512, 256, 128))
    tk_down = _pick(hidden, (2816, 512, 256, 128))

    xf = x.reshape(T, D)
    xn, cosf, sins, w_mix, w_gate = _prologue(xf, g_mix_pre[0].reshape(1, D), positions, head_dim, w_in, n_mix)
    merge_weights = (w_pool_proj, w_conv_proj, w_ret_proj, w_out)
    ffn_weights = (w_ffn_in, w_ffn_out)
    for l in range(depth):
        br, wp16, wc16, wr16, wo16 = _mixers(
            xn, w_mix, l, cosf, sins, pool_w, pool_scale[l], conv_dw[l], conv_b[l], conv_ln_g[l], conv_ln_b[l],
            ret_gn_g[l], merge_weights, seq_len=S, tm=tm_mix, chunk=chunk)
        gates = _proj_gate(xn, w_gate, tm=tm_gate, tn=tn_gate)
        xf, xn, w_mix, w_gate, wfi16, wfo16 = _merge(
            br, gates, xf, wp16, wc16, wr16, wo16, l, g_mix_post[l].reshape(1, D), g_ffn_pre[l].reshape(1, D),
            w_in, n_mix, ffn_weights, tm=tm_merge)
        g_next = g_mix_pre[l + 1].reshape(1, D) if l + 1 < depth else None
        hid = _ffn_up(xn, wfi16, tm=tm_up, tn=tn_up)
        xf, xn = _ffn_down(hid, xf, wfo16, g_ffn_post[l].reshape(1, D), g_next, tm=tm_down, tk=tk_down)
    return xf.reshape(B, S, D)
```

```python
import functools
import math

import jax
import jax.numpy as jnp
from jax import lax
from jax.experimental import pallas as pl
from jax.experimental.pallas import tpu as pltpu

F32 = jnp.float32
BF16 = jnp.bfloat16

NORM_EPS = 1e-6
LN_EPS = 1e-5
ROPE_BASE = 10000.0
POOL_WINDOWS = (2, 4, 8, 16)
RET_HEADS = 8
N_BRANCH = 3

LANES = 128
SUBLANES = 8
BF16_ROWS = 16
VMEM_LIMIT_BYTES = 56 * 1024 * 1024

POOL_HALO = 16
CONV_HALO = 32
CONV_ROWS = 64
RET_HEAD_GROUP = 4
GATE_ROWS = 512
MIX_ROWS = 512
RET_CHUNK = 256


def _rms_norm(x, g):
    return x * lax.rsqrt(jnp.mean(x * x, axis=-1, keepdims=True) + NORM_EPS) * g


def _sigmoid(x):
    return 0.5 * jnp.tanh(0.5 * x) + 0.5


def _compiler_params(semantics):
    return pltpu.CompilerParams(dimension_semantics=semantics, vmem_limit_bytes=VMEM_LIMIT_BYTES)


def _pick(n, candidates):
    for c in candidates:
        if n % c == 0:
            return c
    raise ValueError(f"no tile in {candidates} divides {n}")


def _rope(pos, half):
    lane = lax.broadcasted_iota(jnp.int32, (1, 2 * half), 1)
    freq = (lane % half).astype(F32)
    inv = jnp.exp(freq * (-math.log(ROPE_BASE) / half))
    ang = pos * inv
    sin = jnp.sin(ang)
    return jnp.cos(ang), jnp.where(lane < half, -sin, sin)


def _cast_specs(w, layer, n_steps, period=1):
    _, R, C = w.shape
    rows = R * period // n_steps
    assert R * period % n_steps == 0 and rows % BF16_ROWS == 0 and C % LANES == 0
    in_spec = pl.BlockSpec((None, rows, C), lambda i: (layer, i // period, 0))
    out_spec = pl.BlockSpec((rows, C), lambda i: (i // period, 0))
    return in_spec, out_spec, jax.ShapeDtypeStruct((R, C), BF16)


def _cast_period(w, n_steps):
    period = 1
    while (w.shape[1] * period) % (n_steps * BF16_ROWS):
        period *= 2
    return period


def _split_cast_specs(w, layer, n_steps, n_first):
    _, R, C = w.shape
    rows = R // n_steps
    assert R % n_steps == 0 and rows % BF16_ROWS == 0 and n_first % LANES == 0 and C % LANES == 0
    in_spec = pl.BlockSpec((None, rows, C), lambda i: (layer, i, 0))
    out_specs = [pl.BlockSpec((rows, n_first), lambda i: (i, 0)), pl.BlockSpec((rows, C - n_first), lambda i: (i, 0))]
    out_shapes = [jax.ShapeDtypeStruct((R, n_first), BF16), jax.ShapeDtypeStruct((R, C - n_first), BF16)]
    return in_spec, out_specs, out_shapes


def _split_cast(w_ref, first_ref, second_ref):
    n_first = first_ref.shape[1]
    first_ref[...] = w_ref[:, 0:n_first].astype(first_ref.dtype)
    second_ref[...] = w_ref[:, n_first:].astype(second_ref.dtype)


def _prologue_kernel(x_ref, g_ref, pos_ref, w_ref, o_ref, cos_ref, sin_ref, wmix_ref, wgate_ref):
    o_ref[...] = _rms_norm(x_ref[...], g_ref[...]).astype(o_ref.dtype)
    cos_ref[...], sin_ref[...] = _rope(pos_ref[...], pos_ref.shape[1] // 2)
    _split_cast(w_ref, wmix_ref, wgate_ref)


def _prologue(x, g, positions, head_dim, w_in, n_mix):
    T, D = x.shape
    tm = _pick(T, (512, 256, 128))
    posb = jnp.broadcast_to(positions.astype(F32).reshape(T, 1), (T, head_dim))
    w_spec, w_out_specs, w_out_shapes = _split_cast_specs(w_in, 0, T // tm, n_mix)
    rows = pl.BlockSpec((tm, D), lambda i: (i, 0))
    rope = pl.BlockSpec((tm, head_dim), lambda i: (i, 0))
    return pl.pallas_call(
        _prologue_kernel,
        grid=(T // tm,),
        in_specs=[rows, pl.BlockSpec((1, D), lambda i: (0, 0)), rope, w_spec],
        out_specs=[rows, rope, rope] + w_out_specs,
        out_shape=[jax.ShapeDtypeStruct((T, D), BF16)] + [jax.ShapeDtypeStruct((T, head_dim), F32)] * 2
        + w_out_shapes,
        compiler_params=_compiler_params(("parallel",)),
        name="prologue",
    )(x, g, posb, w_in)


def _proj_gate_kernel(x_ref, w_ref, o_ref):
    for r0 in range(0, x_ref.shape[0], GATE_ROWS):
        rows = slice(r0, min(r0 + GATE_ROWS, x_ref.shape[0]))
        acc = jnp.dot(x_ref[rows, :], w_ref[...], preferred_element_type=F32)
        o_ref[rows, :] = _sigmoid(acc).astype(o_ref.dtype)


def _proj_gate(xn, w, *, tm, tn):
    T, D = xn.shape
    N = w.shape[1]
    return pl.pallas_call(
        _proj_gate_kernel,
        grid=(T // tm, N // tn),
        in_specs=[
            pl.BlockSpec((tm, D), lambda i, j: (i, 0)),
            pl.BlockSpec((D, tn), lambda i, j: (0, j)),
        ],
        out_specs=pl.BlockSpec((tm, tn), lambda i, j: (i, j)),
        out_shape=jax.ShapeDtypeStruct((T, N), BF16),
        compiler_params=_compiler_params(("parallel", "arbitrary")),
        name="proj_gate",
    )(xn, w)


def _rotary(t, cos, sin):
    return t * cos + pltpu.roll(t, t.shape[1] // 2, axis=1) * sin


def _mixer_kernel(x_ref, w_ref, cos_ref, sin_ref, poolw_ref, pscale_ref, dw_ref, cb_ref, lng_ref, lnb_ref,
                  gng_ref, *rest, n_casts, tm, chunk, tiles_per_seq, pool_width, conv_width, conv_kernel,
                  ret_width):
    cast_in, out_ref, cast_out = rest[:n_casts], rest[n_casts], rest[n_casts + 1:2 * n_casts + 1]
    u_scr, h_scr, state_scr, decay_scr, xi_scr, zeta_scr = rest[2 * n_casts + 1:]
    for src_ref, dst_ref in zip(cast_in, cast_out):
        dst_ref[...] = src_ref[...].astype(dst_ref.dtype)

    tile = pl.program_id(0)
    s = tile % tiles_per_seq
    heads = RET_HEADS
    dh = ret_width // heads
    pool_group = pool_width // len(POOL_WINDOWS)
    log_gamma = [math.log1p(-(2.0 ** (-5.0 - h))) for h in range(heads)]

    off_a = pool_width
    off_g = off_a + conv_width
    off_q = off_g + conv_width
    off_k = off_q + ret_width
    off_v = off_k + ret_width
    off_gate = off_v + ret_width

    @pl.when(tile == 0)
    def _():
        row = lax.broadcasted_iota(jnp.int32, (chunk, chunk), 0).astype(F32)
        col = lax.broadcasted_iota(jnp.int32, (chunk, chunk), 1).astype(F32)
        rel = row - col
        pos = lax.broadcasted_iota(jnp.int32, (chunk, dh), 0).astype(F32)
        for h in range(heads):
            decay_scr[h] = jnp.where(rel >= 0, jnp.exp(jnp.maximum(rel, 0.0) * log_gamma[h]), 0.0)
            xi_scr[h] = jnp.exp((pos + 1.0) * log_gamma[h])
            zeta_scr[h] = jnp.exp((chunk - 1.0 - pos) * log_gamma[h])

    @pl.when(s == 0)
    def _():
        u_scr[0:POOL_HALO, :] = jnp.zeros((POOL_HALO, pool_width), F32)
        h_scr[0:CONV_HALO, :] = jnp.zeros((CONV_HALO, conv_width), F32)
        state_scr[...] = jnp.zeros_like(state_scr)

    for r_off in range(0, tm, MIX_ROWS):
        sub = slice(r_off, r_off + MIX_ROWS)

        def stream(c0, width, sub=sub):
            return jnp.dot(x_ref[sub, :], w_ref[:, c0:c0 + width], preferred_element_type=F32)

        u_scr[POOL_HALO + r_off:POOL_HALO + r_off + MIX_ROWS, :] = stream(0, pool_width)
        t_idx = s * tm + r_off + lax.broadcasted_iota(jnp.int32, (MIX_ROWS, 1), 0)
        for gi, w in enumerate(POOL_WINDOWS):
            c0 = gi * pool_group
            ug = u_scr[r_off:r_off + POOL_HALO + MIX_ROWS, c0:c0 + pool_group]
            win = ug
            span = 1
            while span < w:
                win = win + pltpu.roll(win, span, axis=0)
                span *= 2
            cnt = jnp.minimum(t_idx + 1, w).astype(F32)
            p = win[POOL_HALO:, :] / cnt - ug[POOL_HALO:, :]
            y = jnp.dot(p.astype(BF16), poolw_ref[gi].astype(BF16), preferred_element_type=F32)
            out_ref[sub, c0:c0 + pool_group] = (y * pscale_ref[:, c0:c0 + pool_group]).astype(out_ref.dtype)

        h_scr[CONV_HALO + r_off:CONV_HALO + r_off + MIX_ROWS, :] = (
            stream(off_a, conv_width) * _sigmoid(stream(off_g, conv_width)))
        win_rows = CONV_ROWS + SUBLANES
        for r0 in range(r_off, r_off + MIX_ROWS, CONV_ROWS):
            base = CONV_HALO + r0 - SUBLANES
            blocks = []
            for c0 in range(0, conv_width, LANES):
                cols = slice(c0, c0 + LANES)
                acc = None
                for r in range(SUBLANES):
                    part = None
                    for a in range((conv_kernel - 1 - r) // SUBLANES + 1):
                        j = conv_kernel - 1 - (SUBLANES * a + r)
                        lo = base - SUBLANES * a
                        term = dw_ref[j:j + 1, cols] * h_scr[lo:lo + win_rows, cols]
                        part = term if part is None else part + term
                    if r:
                        part = pltpu.roll(part, r, axis=0)
                    acc = part if acc is None else acc + part
                blocks.append(acc[SUBLANES:, :] + cb_ref[:, cols])
            conv = jnp.concatenate(blocks, axis=1)
            mu = jnp.mean(conv, axis=-1, keepdims=True)
            d = conv - mu
            var = jnp.mean(d * d, axis=-1, keepdims=True)
            y = d * lax.rsqrt(var + LN_EPS) * lng_ref[...] + lnb_ref[...]
            out_ref[r0:r0 + CONV_ROWS, pool_width:pool_width + conv_width] = (
                y * _sigmoid(y)).astype(out_ref.dtype)

        out_off = pool_width + conv_width
        cosf = cos_ref[sub, :]
        sins = sin_ref[sub, :]
        group = RET_HEAD_GROUP
        for h0 in range(0, heads, group):
            width = group * dh
            q_g = stream(off_q + h0 * dh, width)
            k_g = stream(off_k + h0 * dh, width)
            v_g = stream(off_v + h0 * dh, width)
            gate_g = stream(off_gate + h0 * dh, width)
            for hh in range(group):
                h = h0 + hh
                lanes = slice(hh * dh, (hh + 1) * dh)
                gamma_c = math.exp(chunk * log_gamma[h])
                qr = _rotary(q_g[:, lanes], cosf, sins).astype(BF16)
                kr = (_rotary(k_g[:, lanes], cosf, sins) * (dh ** -0.5)).astype(BF16)
                for c in range(MIX_ROWS // chunk):
                    rows = slice(c * chunk, (c + 1) * chunk)
                    q = qr[rows]
                    k = kr[rows]
                    v = v_g[rows, lanes]
                    scores = lax.dot_general(q, k, (((1,), (1,)), ((), ())), preferred_element_type=F32)
                    scores = scores * decay_scr[h]
                    intra = jnp.dot(scores.astype(BF16), v.astype(BF16), preferred_element_type=F32)
                    state = state_scr[h]
                    inter = jnp.dot(q, state.astype(BF16), preferred_element_type=F32) * xi_scr[h]
                    kv = lax.dot_general(k, (v * zeta_scr[h]).astype(BF16), (((0,), (0,)), ((), ())),
                                         preferred_element_type=F32)
                    state_scr[h] = gamma_c * state + kv
                    o = intra + inter
                    mu = jnp.mean(o, axis=-1, keepdims=True)
                    d = o - mu
                    var = jnp.mean(d * d, axis=-1, keepdims=True)
                    o = d * lax.rsqrt(var + LN_EPS) * gng_ref[:, h * dh:(h + 1) * dh]
                    gate = gate_g[rows, lanes]
                    out_rows = slice(r_off + c * chunk, r_off + (c + 1) * chunk)
                    out_ref[out_rows, out_off + h * dh:out_off + (h + 1) * dh] = (
                        gate * _sigmoid(gate) * o).astype(out_ref.dtype)

    u_scr[0:POOL_HALO, :] = u_scr[tm:tm + POOL_HALO, :]
    h_scr[0:CONV_HALO, :] = h_scr[tm:tm + CONV_HALO, :]


def _mixers(xn, w_mix, layer, cosf, sins, pool_w, pool_scale, conv_dw, conv_b, ln_g, ln_b, gn_g, cast_weights,
            *, seq_len, tm, chunk):
    T, d_model = xn.shape
    pool_width = pool_scale.shape[-1]
    conv_kernel, conv_width = conv_dw.shape
    ret_width = gn_g.shape[-1]
    n_mix = pool_width + 2 * conv_width + 4 * ret_width
    dh = ret_width // RET_HEADS
    n_groups = len(POOL_WINDOWS)
    pool_group = pool_width // n_groups
    n_steps = T // tm
    assert pool_width + conv_width + ret_width == d_model and w_mix.shape == (d_model, n_mix)
    assert dh == LANES and pool_group == LANES and conv_width % LANES == 0 and RET_HEADS % RET_HEAD_GROUP == 0
    assert chunk % SUBLANES == 0 and MIX_ROWS % chunk == 0 and MIX_ROWS % CONV_ROWS == 0
    assert tm % MIX_ROWS == 0 and seq_len % tm == 0
    assert max(POOL_WINDOWS) - 1 <= POOL_HALO and SUBLANES * ((conv_kernel - 1) // SUBLANES + 1) <= CONV_HALO

    def row(v):
        return v.reshape(1, -1)

    def const_spec(shape):
        return pl.BlockSpec(shape, lambda i: (0,) * len(shape))

    casts = [_cast_specs(w, layer, n_steps, _cast_period(w, n_steps)) for w in cast_weights]

    kern = functools.partial(_mixer_kernel, n_casts=len(casts), tm=tm, chunk=chunk, tiles_per_seq=seq_len // tm,
                             pool_width=pool_width, conv_width=conv_width, conv_kernel=conv_kernel,
                             ret_width=ret_width)
    return pl.pallas_call(
        kern,
        grid=(n_steps,),
        in_specs=[
            pl.BlockSpec((tm, d_model), lambda i: (i, 0)),
            pl.BlockSpec((d_model, n_mix), lambda i: (0, 0), pipeline_mode=pl.Buffered(1)),
            pl.BlockSpec((tm, dh), lambda i: (i, 0)),
            pl.BlockSpec((tm, dh), lambda i: (i, 0)),
            pl.BlockSpec((None, n_groups, pool_group, pool_group), lambda i: (layer, 0, 0, 0)),
            const_spec((1, pool_width)),
            const_spec((conv_kernel, conv_width)),
            const_spec((1, conv_width)),
            const_spec((1, conv_width)),
            const_spec((1, conv_width)),
            const_spec((1, ret_width)),
        ] + [c[0] for c in casts],
        out_specs=[pl.BlockSpec((tm, d_model), lambda i: (i, 0))] + [c[1] for c in casts],
        out_shape=[jax.ShapeDtypeStruct((T, d_model), BF16)] + [c[2] for c in casts],
        scratch_shapes=[
            pltpu.VMEM((POOL_HALO + tm, pool_width), F32),
            pltpu.VMEM((CONV_HALO + tm, conv_width), F32),
            pltpu.VMEM((RET_HEADS, dh, dh), F32),
            pltpu.VMEM((RET_HEADS, chunk, chunk), F32),
            pltpu.VMEM((RET_HEADS, chunk, dh), F32),
            pltpu.VMEM((RET_HEADS, chunk, dh), F32),
        ],
        compiler_params=_compiler_params(("arbitrary",)),
        name="mixers",
    )(xn, w_mix, cosf, sins, pool_w, row(pool_scale), conv_dw, row(conv_b), row(ln_g), row(ln_b), row(gn_g),
      *cast_weights)


def _merge_kernel(br_ref, gate_ref, x_ref, wp_ref, wc_ref, wr_ref, wo_ref, g_ref, gn_ref, *rest,
                  pool_width, conv_width, cast_next, n_casts):
    n_in = n_casts + (1 if cast_next else 0)
    ins, outs = rest[:n_in], rest[n_in:]
    o_ref, xn_ref = outs[0], outs[1]
    if cast_next:
        _split_cast(ins[0], outs[2], outs[3])
    for src_ref, dst_ref in zip(ins[n_in - n_casts:], outs[len(outs) - n_casts:]):
        dst_ref[...] = src_ref[...].astype(dst_ref.dtype)
    d = x_ref.shape[-1]
    c1 = pool_width
    c2 = pool_width + conv_width
    y_pool = jnp.dot(br_ref[:, 0:c1], wp_ref[...], preferred_element_type=F32)
    merged = gate_ref[:, 0:d].astype(F32) * y_pool
    y_conv = jnp.dot(br_ref[:, c1:c2], wc_ref[...], preferred_element_type=F32)
    merged = merged + gate_ref[:, d:2 * d].astype(F32) * y_conv
    y_ret = jnp.dot(br_ref[:, c2:d], wr_ref[...], preferred_element_type=F32)
    merged = merged + gate_ref[:, 2 * d:3 * d].astype(F32) * y_ret
    out = jnp.dot(merged.astype(BF16), wo_ref[...], preferred_element_type=F32)
    x_new = x_ref[...] + _rms_norm(out, g_ref[...])
    o_ref[...] = x_new
    xn_ref[...] = _rms_norm(x_new, gn_ref[...]).astype(xn_ref.dtype)


def _merge(br, gates, x, wp, wc, wr, wo, layer, g_post, g_next, w_in, n_mix, cast_weights, *, tm):
    T, D = x.shape
    pool_width, conv_width, ret_width = wp.shape[0], wc.shape[0], wr.shape[0]
    cast_next = layer + 1 < w_in.shape[0]

    def resident(rows):
        return pl.BlockSpec((rows, D), lambda i: (0, 0), pipeline_mode=pl.Buffered(1))

    def vec():
        return pl.BlockSpec((1, D), lambda i: (0, 0))

    def rows(width):
        return pl.BlockSpec((tm, width), lambda i: (i, 0))

    in_specs = [rows(D), rows(N_BRANCH * D), rows(D), resident(pool_width), resident(conv_width),
                resident(ret_width), resident(D), vec(), vec()]
    args = [br, gates, x, wp, wc, wr, wo, g_post, g_next]
    out_specs = [rows(D), rows(D)]
    out_shape = [jax.ShapeDtypeStruct((T, D), F32), jax.ShapeDtypeStruct((T, D), BF16)]
    if cast_next:
        w_spec, w_out_specs, w_out_shapes = _split_cast_specs(w_in, layer + 1, T // tm, n_mix)
        in_specs.append(w_spec)
        args.append(w_in)
        out_specs += w_out_specs
        out_shape += w_out_shapes
    for w in cast_weights:
        c_in, c_out, c_shape = _cast_specs(w, layer, T // tm, _cast_period(w, T // tm))
        in_specs.append(c_in)
        args.append(w)
        out_specs.append(c_out)
        out_shape.append(c_shape)
    outs = pl.pallas_call(
        functools.partial(_merge_kernel, pool_width=pool_width, conv_width=conv_width, cast_next=cast_next,
                          n_casts=len(cast_weights)),
        grid=(T // tm,),
        in_specs=in_specs,
        out_specs=out_specs,
        out_shape=out_shape,
        compiler_params=_compiler_params(("parallel",)),
        name="merge",
    )(*args)
    next_w = tuple(outs[2:4]) if cast_next else (None, None)
    return (outs[0], outs[1]) + next_w + tuple(outs[len(outs) - len(cast_weights):])


def _ffn_up_kernel(x_ref, wa_ref, wb_ref, o_ref):
    for r0 in range(0, x_ref.shape[0], GATE_ROWS):
        rows = slice(r0, min(r0 + GATE_ROWS, x_ref.shape[0]))
        xn = x_ref[rows, :]
        a = jnp.dot(xn, wa_ref[...], preferred_element_type=F32)
        b = jnp.dot(xn, wb_ref[...], preferred_element_type=F32)
        o_ref[rows, :] = (a * _sigmoid(a) * b).astype(o_ref.dtype)


def _ffn_up(xn, w_in, *, tm, tn):
    T, D = xn.shape
    hidden = w_in.shape[1] // 2
    nh = hidden // tn
    return pl.pallas_call(
        _ffn_up_kernel,
        grid=(T // tm, nh),
        in_specs=[
            pl.BlockSpec((tm, D), lambda i, j: (i, 0)),
            pl.BlockSpec((D, tn), lambda i, j: (0, j)),
            pl.BlockSpec((D, tn), lambda i, j: (0, j + nh)),
        ],
        out_specs=pl.BlockSpec((tm, tn), lambda i, j: (i, j)),
        out_shape=jax.ShapeDtypeStruct((T, hidden), BF16),
        compiler_params=_compiler_params(("parallel", "arbitrary")),
        name="ffn_up",
    )(xn, w_in, w_in)


def _ffn_down_kernel(h_ref, x_ref, wo_ref, gpost_ref, *rest, emit_next):
    if emit_next:
        gnext_ref, o_ref, xn_out_ref = rest
    else:
        (o_ref,) = rest
    k = pl.program_id(1)

    @pl.when(k == 0)
    def _():
        o_ref[...] = jnp.zeros_like(o_ref)

    o_ref[...] += jnp.dot(h_ref[...], wo_ref[...], preferred_element_type=F32)

    @pl.when(k == pl.num_programs(1) - 1)
    def _():
        x_new = x_ref[...] + _rms_norm(o_ref[...], gpost_ref[...])
        o_ref[...] = x_new
        if emit_next:
            xn_out_ref[...] = _rms_norm(x_new, gnext_ref[...]).astype(xn_out_ref.dtype)


def _ffn_down(hid, x, w_out, g_post, g_next, *, tm, tk):
    T, D = x.shape
    hidden = w_out.shape[0]
    emit_next = g_next is not None

    def rows():
        return pl.BlockSpec((tm, D), lambda i, k: (i, 0))

    def vec():
        return pl.BlockSpec((1, D), lambda i, k: (0, 0))

    in_specs = [pl.BlockSpec((tm, tk), lambda i, k: (i, k)), rows(), pl.BlockSpec((tk, D), lambda i, k: (k, 0)), vec()]
    args = [hid, x, w_out, g_post]
    out_specs = [rows()]
    out_shape = [jax.ShapeDtypeStruct((T, D), F32)]
    if emit_next:
        in_specs.append(vec())
        args.append(g_next)
        out_specs.append(rows())
        out_shape.append(jax.ShapeDtypeStruct((T, D), BF16))
    outs = pl.pallas_call(
        functools.partial(_ffn_down_kernel, emit_next=emit_next),
        grid=(T // tm, hidden // tk),
        in_specs=in_specs,
        out_specs=out_specs,
        out_shape=out_shape,
        compiler_params=_compiler_params(("parallel", "arbitrary")),
        name="ffn_down",
    )(*args)
    return (outs[0], outs[1]) if emit_next else (outs[0], None)


def kernel(x, positions, g_mix_pre, g_mix_post, g_ffn_pre, g_ffn_post, w_in, pool_w, pool_scale, conv_dw,
           conv_b, conv_ln_g, conv_ln_b, ret_gn_g, w_pool_proj, w_conv_proj, w_ret_proj, w_out, w_ffn_in,
           w_ffn_out):
    B, S, D = x.shape
    depth = w_in.shape[0]
    T = B * S
    pool_width = pool_scale.shape[-1]
    conv_width = conv_b.shape[-1]
    ret_width = ret_gn_g.shape[-1]
    n_mix = pool_width + 2 * conv_width + 4 * ret_width
    hidden = w_ffn_out.shape[1]
    head_dim = ret_width // RET_HEADS
    chunk = RET_CHUNK

    tm_mix = _pick(S, (MIX_ROWS,))
    tm_gate = _pick(T, (2048, 1024, 512, 256, 128))
    tn_gate = _pick(N_BRANCH * D, (2048, 1024, 512, 256, 128))
    tm_merge = _pick(T, (256, 128))
    tm_up = _pick(T, (2048, 1024, 512, 256, 128))
    tn_up = _pick(hidden, (512, 256, 128))
    tm_down = _pick(T, (512, 256, 128))
    tk_down = _pick(hidden, (2816, 512, 256, 128))

    xf = x.reshape(T, D)
    xn, cosf, sins, w_mix, w_gate = _prologue(xf, g_mix_pre[0].reshape(1, D), positions, head_dim, w_in, n_mix)
    merge_weights = (w_pool_proj, w_conv_proj, w_ret_proj, w_out)
    ffn_weights = (w_ffn_in, w_ffn_out)
    for l in range(depth):
        br, wp16, wc16, wr16, wo16 = _mixers(
            xn, w_mix, l, cosf, sins, pool_w, pool_scale[l], conv_dw[l], conv_b[l], conv_ln_g[l], conv_ln_b[l],
            ret_gn_g[l], merge_weights, seq_len=S, tm=tm_mix, chunk=chunk)
        gates = _proj_gate(xn, w_gate, tm=tm_gate, tn=tn_gate)
        xf, xn, w_mix, w_gate, wfi16, wfo16 = _merge(
            br, gates, xf, wp16, wc16, wr16, wo16, l, g_mix_post[l].reshape(1, D), g_ffn_pre[l].reshape(1, D),
            w_in, n_mix, ffn_weights, tm=tm_merge)
        g_next = g_mix_pre[l + 1].reshape(1, D) if l + 1 < depth else None
        hid = _ffn_up(xn, wfi16, tm=tm_up, tn=tn_up)
        xf, xn = _ffn_down(hid, xf, wfo16, g_ffn_post[l].reshape(1, D), g_next, tm=tm_down, tk=tk_down)
    return xf.reshape(B, S, D)
```

```python
import functools
import math

import jax
import jax.numpy as jnp
from jax import lax
from jax.experimental import pallas as pl
from jax.experimental.pallas import tpu as pltpu

F32 = jnp.float32
BF16 = jnp.bfloat16

NORM_EPS = 1e-6
LN_EPS = 1e-5
ROPE_BASE = 10000.0
POOL_WINDOWS = (2, 4, 8, 16)
RET_HEADS = 8
N_BRANCH = 3

LANES = 128
SUBLANES = 8
BF16_ROWS = 16
VMEM_LIMIT_BYTES = 56 * 1024 * 1024

POOL_HALO = 16
CONV_HALO = 32
CONV_ROWS = 64
RET_HEAD_GROUP = 4
GATE_ROWS = 1024
MIX_ROWS = 512
RET_CHUNK = 256


def _rms_norm(x, g):
    return x * lax.rsqrt(jnp.mean(x * x, axis=-1, keepdims=True) + NORM_EPS) * g


def _sigmoid(x):
    return 0.5 * jnp.tanh(0.5 * x) + 0.5


def _compiler_params(semantics):
    return pltpu.CompilerParams(dimension_semantics=semantics, vmem_limit_bytes=VMEM_LIMIT_BYTES)


def _pick(n, candidates):
    for c in candidates:
        if n % c == 0:
            return c
    raise ValueError(f"no tile in {candidates} divides {n}")


def _rope(pos, half):
    lane = lax.broadcasted_iota(jnp.int32, (1, 2 * half), 1)
    freq = (lane % half).astype(F32)
    inv = jnp.exp(freq * (-math.log(ROPE_BASE) / half))
    ang = pos * inv
    sin = jnp.sin(ang)
    return jnp.cos(ang), jnp.where(lane < half, -sin, sin)


def _cast_specs(w, layer, n_steps, period=1):
    _, R, C = w.shape
    rows = R * period // n_steps
    assert R * period % n_steps == 0 and rows % BF16_ROWS == 0 and C % LANES == 0
    in_spec = pl.BlockSpec((None, rows, C), lambda i: (layer, i // period, 0))
    out_spec = pl.BlockSpec((rows, C), lambda i: (i // period, 0))
    return in_spec, out_spec, jax.ShapeDtypeStruct((R, C), BF16)


def _cast_period(w, n_steps):
    period = 1
    while (w.shape[1] * period) % (n_steps * BF16_ROWS):
        period *= 2
    return period


def _split_cast_specs(w, layer, n_steps, n_first):
    _, R, C = w.shape
    rows = R // n_steps
    assert R % n_steps == 0 and rows % BF16_ROWS == 0 and n_first % LANES == 0 and C % LANES == 0
    in_spec = pl.BlockSpec((None, rows, C), lambda i: (layer, i, 0))
    out_specs = [pl.BlockSpec((rows, n_first), lambda i: (i, 0)), pl.BlockSpec((rows, C - n_first), lambda i: (i, 0))]
    out_shapes = [jax.ShapeDtypeStruct((R, n_first), BF16), jax.ShapeDtypeStruct((R, C - n_first), BF16)]
    return in_spec, out_specs, out_shapes


def _split_cast(w_ref, first_ref, second_ref):
    n_first = first_ref.shape[1]
    first_ref[...] = w_ref[:, 0:n_first].astype(first_ref.dtype)
    second_ref[...] = w_ref[:, n_first:].astype(second_ref.dtype)


def _prologue_kernel(x_ref, g_ref, pos_ref, w_ref, o_ref, cos_ref, sin_ref, wmix_ref, wgate_ref):
    o_ref[...] = _rms_norm(x_ref[...], g_ref[...]).astype(o_ref.dtype)
    cos_ref[...], sin_ref[...] = _rope(pos_ref[...], pos_ref.shape[1] // 2)
    _split_cast(w_ref, wmix_ref, wgate_ref)


def _prologue(x, g, positions, head_dim, w_in, n_mix):
    T, D = x.shape
    tm = _pick(T, (512, 256, 128))
    posb = jnp.broadcast_to(positions.astype(F32).reshape(T, 1), (T, head_dim))
    w_spec, w_out_specs, w_out_shapes = _split_cast_specs(w_in, 0, T // tm, n_mix)
    rows = pl.BlockSpec((tm, D), lambda i: (i, 0))
    rope = pl.BlockSpec((tm, head_dim), lambda i: (i, 0))
    return pl.pallas_call(
        _prologue_kernel,
        grid=(T // tm,),
        in_specs=[rows, pl.BlockSpec((1, D), lambda i: (0, 0)), rope, w_spec],
        out_specs=[rows, rope, rope] + w_out_specs,
        out_shape=[jax.ShapeDtypeStruct((T, D), BF16)] + [jax.ShapeDtypeStruct((T, head_dim), F32)] * 2
        + w_out_shapes,
        compiler_params=_compiler_params(("parallel",)),
        name="prologue",
    )(x, g, posb, w_in)


def _proj_gate_kernel(x_ref, w_ref, o_ref):
    for r0 in range(0, x_ref.shape[0], GATE_ROWS):
        rows = slice(r0, min(r0 + GATE_ROWS, x_ref.shape[0]))
        acc = jnp.dot(x_ref[rows, :], w_ref[...], preferred_element_type=F32)
        o_ref[rows, :] = _sigmoid(acc).astype(o_ref.dtype)


def _proj_gate(xn, w, *, tm, tn):
    T, D = xn.shape
    N = w.shape[1]
    return pl.pallas_call(
        _proj_gate_kernel,
        grid=(T // tm, N // tn),
        in_specs=[
            pl.BlockSpec((tm, D), lambda i, j: (i, 0)),
            pl.BlockSpec((D, tn), lambda i, j: (0, j)),
        ],
        out_specs=pl.BlockSpec((tm, tn), lambda i, j: (i, j)),
        out_shape=jax.ShapeDtypeStruct((T, N), BF16),
        compiler_params=_compiler_params(("parallel", "arbitrary")),
        name="proj_gate",
    )(xn, w)


def _rotary(t, cos, sin):
    return t * cos + pltpu.roll(t, t.shape[1] // 2, axis=1) * sin


def _mixer_kernel(x_ref, w_ref, cos_ref, sin_ref, poolw_ref, pscale_ref, dw_ref, cb_ref, lng_ref, lnb_ref,
                  gng_ref, *rest, n_casts, tm, chunk, tiles_per_seq, pool_width, conv_width, conv_kernel,
                  ret_width):
    cast_in, out_ref, cast_out = rest[:n_casts], rest[n_casts], rest[n_casts + 1:2 * n_casts + 1]
    u_scr, h_scr, state_scr, decay_scr, xi_scr, zeta_scr = rest[2 * n_casts + 1:]
    for src_ref, dst_ref in zip(cast_in, cast_out):
        dst_ref[...] = src_ref[...].astype(dst_ref.dtype)

    tile = pl.program_id(0)
    s = tile % tiles_per_seq
    heads = RET_HEADS
    dh = ret_width // heads
    pool_group = pool_width // len(POOL_WINDOWS)
    log_gamma = [math.log1p(-(2.0 ** (-5.0 - h))) for h in range(heads)]

    off_a = pool_width
    off_g = off_a + conv_width
    off_q = off_g + conv_width
    off_k = off_q + ret_width
    off_v = off_k + ret_width
    off_gate = off_v + ret_width

    @pl.when(tile == 0)
    def _():
        row = lax.broadcasted_iota(jnp.int32, (chunk, chunk), 0).astype(F32)
        col = lax.broadcasted_iota(jnp.int32, (chunk, chunk), 1).astype(F32)
        rel = row - col
        pos = lax.broadcasted_iota(jnp.int32, (chunk, dh), 0).astype(F32)
        for h in range(heads):
            decay_scr[h] = jnp.where(rel >= 0, jnp.exp(jnp.maximum(rel, 0.0) * log_gamma[h]), 0.0)
            xi_scr[h] = jnp.exp((pos + 1.0) * log_gamma[h])
            zeta_scr[h] = jnp.exp((chunk - 1.0 - pos) * log_gamma[h])

    @pl.when(s == 0)
    def _():
        u_scr[0:POOL_HALO, :] = jnp.zeros((POOL_HALO, pool_width), F32)
        h_scr[0:CONV_HALO, :] = jnp.zeros((CONV_HALO, conv_width), F32)
        state_scr[...] = jnp.zeros_like(state_scr)

    for r_off in range(0, tm, MIX_ROWS):
        sub = slice(r_off, r_off + MIX_ROWS)

        def stream(c0, width, sub=sub):
            return jnp.dot(x_ref[sub, :], w_ref[:, c0:c0 + width], preferred_element_type=F32)

        u_scr[POOL_HALO + r_off:POOL_HALO + r_off + MIX_ROWS, :] = stream(0, pool_width)
        t_idx = s * tm + r_off + lax.broadcasted_iota(jnp.int32, (MIX_ROWS, 1), 0)
        for gi, w in enumerate(POOL_WINDOWS):
            c0 = gi * pool_group
            ug = u_scr[r_off:r_off + POOL_HALO + MIX_ROWS, c0:c0 + pool_group]
            win = ug
            span = 1
            while span < w:
                win = win + pltpu.roll(win, span, axis=0)
                span *= 2
            cnt = jnp.minimum(t_idx + 1, w).astype(F32)
            p = win[POOL_HALO:, :] / cnt - ug[POOL_HALO:, :]
            y = jnp.dot(p.astype(BF16), poolw_ref[gi].astype(BF16), preferred_element_type=F32)
            out_ref[sub, c0:c0 + pool_group] = (y * pscale_ref[:, c0:c0 + pool_group]).astype(out_ref.dtype)

        h_scr[CONV_HALO + r_off:CONV_HALO + r_off + MIX_ROWS, :] = (
            stream(off_a, conv_width) * _sigmoid(stream(off_g, conv_width)))
        win_rows = CONV_ROWS + SUBLANES
        for r0 in range(r_off, r_off + MIX_ROWS, CONV_ROWS):
            base = CONV_HALO + r0 - SUBLANES
            blocks = []
            for c0 in range(0, conv_width, LANES):
                cols = slice(c0, c0 + LANES)
                acc = None
                for r in range(SUBLANES):
                    part = None
                    for a in range((conv_kernel - 1 - r) // SUBLANES + 1):
                        j = conv_kernel - 1 - (SUBLANES * a + r)
                        lo = base - SUBLANES * a
                        term = dw_ref[j:j + 1, cols] * h_scr[lo:lo + win_rows, cols]
                        part = term if part is None else part + term
                    if r:
                        part = pltpu.roll(part, r, axis=0)
                    acc = part if acc is None else acc + part
                blocks.append(acc[SUBLANES:, :] + cb_ref[:, cols])
            conv = jnp.concatenate(blocks, axis=1)
            mu = jnp.mean(conv, axis=-1, keepdims=True)
            d = conv - mu
            var = jnp.mean(d * d, axis=-1, keepdims=True)
            y = d * lax.rsqrt(var + LN_EPS) * lng_ref[...] + lnb_ref[...]
            out_ref[r0:r0 + CONV_ROWS, pool_width:pool_width + conv_width] = (
                y * _sigmoid(y)).astype(out_ref.dtype)

        out_off = pool_width + conv_width
        cosf = cos_ref[sub, :]
        sins = sin_ref[sub, :]
        group = RET_HEAD_GROUP
        for h0 in range(0, heads, group):
            width = group * dh
            q_g = stream(off_q + h0 * dh, width)
            k_g = stream(off_k + h0 * dh, width)
            v_g = stream(off_v + h0 * dh, width)
            gate_g = stream(off_gate + h0 * dh, width)
            for hh in range(group):
                h = h0 + hh
                lanes = slice(hh * dh, (hh + 1) * dh)
                gamma_c = math.exp(chunk * log_gamma[h])
                qr = _rotary(q_g[:, lanes], cosf, sins).astype(BF16)
                kr = (_rotary(k_g[:, lanes], cosf, sins) * (dh ** -0.5)).astype(BF16)
                for c in range(MIX_ROWS // chunk):
                    rows = slice(c * chunk, (c + 1) * chunk)
                    q = qr[rows]
                    k = kr[rows]
                    v = v_g[rows, lanes]
                    scores = lax.dot_general(q, k, (((1,), (1,)), ((), ())), preferred_element_type=F32)
                    scores = scores * decay_scr[h]
                    intra = jnp.dot(scores.astype(BF16), v.astype(BF16), preferred_element_type=F32)
                    state = state_scr[h]
                    inter = jnp.dot(q, state.astype(BF16), preferred_element_type=F32) * xi_scr[h]
                    kv = lax.dot_general(k, (v * zeta_scr[h]).astype(BF16), (((0,), (0,)), ((), ())),
                                         preferred_element_type=F32)
                    state_scr[h] = gamma_c * state + kv
                    o = intra + inter
                    mu = jnp.mean(o, axis=-1, keepdims=True)
                    d = o - mu
                    var = jnp.mean(d * d, axis=-1, keepdims=True)
                    o = d * lax.rsqrt(var + LN_EPS) * gng_ref[:, h * dh:(h + 1) * dh]
                    gate = gate_g[rows, lanes]
                    out_rows = slice(r_off + c * chunk, r_off + (c + 1) * chunk)
                    out_ref[out_rows, out_off + h * dh:out_off + (h + 1) * dh] = (
                        gate * _sigmoid(gate) * o).astype(out_ref.dtype)

    u_scr[0:POOL_HALO, :] = u_scr[tm:tm + POOL_HALO, :]
    h_scr[0:CONV_HALO, :] = h_scr[tm:tm + CONV_HALO, :]


def _mixers(xn, w_mix, layer, cosf, sins, pool_w, pool_scale, conv_dw, conv_b, ln_g, ln_b, gn_g, cast_weights,
            *, seq_len, tm, chunk):
    T, d_model = xn.shape
    pool_width = pool_scale.shape[-1]
    conv_kernel, conv_width = conv_dw.shape
    ret_width = gn_g.shape[-1]
    n_mix = pool_width + 2 * conv_width + 4 * ret_width
    dh = ret_width // RET_HEADS
    n_groups = len(POOL_WINDOWS)
    pool_group = pool_width // n_groups
    n_steps = T // tm
    assert pool_width + conv_width + ret_width == d_model and w_mix.shape == (d_model, n_mix)
    assert dh == LANES and pool_group == LANES and conv_width % LANES == 0 and RET_HEADS % RET_HEAD_GROUP == 0
    assert chunk % SUBLANES == 0 and MIX_ROWS % chunk == 0 and MIX_ROWS % CONV_ROWS == 0
    assert tm % MIX_ROWS == 0 and seq_len % tm == 0
    assert max(POOL_WINDOWS) - 1 <= POOL_HALO and SUBLANES * ((conv_kernel - 1) // SUBLANES + 1) <= CONV_HALO

    def row(v):
        return v.reshape(1, -1)

    def const_spec(shape):
        return pl.BlockSpec(shape, lambda i: (0,) * len(shape))

    casts = [_cast_specs(w, layer, n_steps, _cast_period(w, n_steps)) for w in cast_weights]

    kern = functools.partial(_mixer_kernel, n_casts=len(casts), tm=tm, chunk=chunk, tiles_per_seq=seq_len // tm,
                             pool_width=pool_width, conv_width=conv_width, conv_kernel=conv_kernel,
                             ret_width=ret_width)
    return pl.pallas_call(
        kern,
        grid=(n_steps,),
        in_specs=[
            pl.BlockSpec((tm, d_model), lambda i: (i, 0)),
            pl.BlockSpec((d_model, n_mix), lambda i: (0, 0), pipeline_mode=pl.Buffered(1)),
            pl.BlockSpec((tm, dh), lambda i: (i, 0)),
            pl.BlockSpec((tm, dh), lambda i: (i, 0)),
            pl.BlockSpec((None, n_groups, pool_group, pool_group), lambda i: (layer, 0, 0, 0)),
            const_spec((1, pool_width)),
            const_spec((conv_kernel, conv_width)),
            const_spec((1, conv_width)),
            const_spec((1, conv_width)),
            const_spec((1, conv_width)),
            const_spec((1, ret_width)),
        ] + [c[0] for c in casts],
        out_specs=[pl.BlockSpec((tm, d_model), lambda i: (i, 0))] + [c[1] for c in casts],
        out_shape=[jax.ShapeDtypeStruct((T, d_model), BF16)] + [c[2] for c in casts],
        scratch_shapes=[
            pltpu.VMEM((POOL_HALO + tm, pool_width), F32),
            pltpu.VMEM((CONV_HALO + tm, conv_width), F32),
            pltpu.VMEM((RET_HEADS, dh, dh), F32),
            pltpu.VMEM((RET_HEADS, chunk, chunk), F32),
            pltpu.VMEM((RET_HEADS, chunk, dh), F32),
            pltpu.VMEM((RET_HEADS, chunk, dh), F32),
        ],
        compiler_params=_compiler_params(("arbitrary",)),
        name="mixers",
    )(xn, w_mix, cosf, sins, pool_w, row(pool_scale), conv_dw, row(conv_b), row(ln_g), row(ln_b), row(gn_g),
      *cast_weights)


def _merge_kernel(br_ref, gate_ref, x_ref, wp_ref, wc_ref, wr_ref, wo_ref, g_ref, gn_ref, *rest,
                  pool_width, conv_width, cast_next, n_casts):
    n_in = n_casts + (1 if cast_next else 0)
    ins, outs = rest[:n_in], rest[n_in:]
    o_ref, xn_ref = outs[0], outs[1]
    if cast_next:
        _split_cast(ins[0], outs[2], outs[3])
    for src_ref, dst_ref in zip(ins[n_in - n_casts:], outs[len(outs) - n_casts:]):
        dst_ref[...] = src_ref[...].astype(dst_ref.dtype)
    d = x_ref.shape[-1]
    c1 = pool_width
    c2 = pool_width + conv_width
    y_pool = jnp.dot(br_ref[:, 0:c1], wp_ref[...], preferred_element_type=F32)
    merged = gate_ref[:, 0:d].astype(F32) * y_pool
    y_conv = jnp.dot(br_ref[:, c1:c2], wc_ref[...], preferred_element_type=F32)
    merged = merged + gate_ref[:, d:2 * d].astype(F32) * y_conv
    y_ret = jnp.dot(br_ref[:, c2:d], wr_ref[...], preferred_element_type=F32)
    merged = merged + gate_ref[:, 2 * d:3 * d].astype(F32) * y_ret
    out = jnp.dot(merged.astype(BF16), wo_ref[...], preferred_element_type=F32)
    x_new = x_ref[...] + _rms_norm(out, g_ref[...])
    o_ref[...] = x_new
    xn_ref[...] = _rms_norm(x_new, gn_ref[...]).astype(xn_ref.dtype)


def _merge(br, gates, x, wp, wc, wr, wo, layer, g_post, g_next, w_in, n_mix, cast_weights, *, tm):
    T, D = x.shape
    pool_width, conv_width, ret_width = wp.shape[0], wc.shape[0], wr.shape[0]
    cast_next = layer + 1 < w_in.shape[0]

    def resident(rows):
        return pl.BlockSpec((rows, D), lambda i: (0, 0), pipeline_mode=pl.Buffered(1))

    def vec():
        return pl.BlockSpec((1, D), lambda i: (0, 0))

    def rows(width):
        return pl.BlockSpec((tm, width), lambda i: (i, 0))

    in_specs = [rows(D), rows(N_BRANCH * D), rows(D), resident(pool_width), resident(conv_width),
                resident(ret_width), resident(D), vec(), vec()]
    args = [br, gates, x, wp, wc, wr, wo, g_post, g_next]
    out_specs = [rows(D), rows(D)]
    out_shape = [jax.ShapeDtypeStruct((T, D), F32), jax.ShapeDtypeStruct((T, D), BF16)]
    if cast_next:
        w_spec, w_out_specs, w_out_shapes = _split_cast_specs(w_in, layer + 1, T // tm, n_mix)
        in_specs.append(w_spec)
        args.append(w_in)
        out_specs += w_out_specs
        out_shape += w_out_shapes
    for w in cast_weights:
        c_in, c_out, c_shape = _cast_specs(w, layer, T // tm, _cast_period(w, T // tm))
        in_specs.append(c_in)
        args.append(w)
        out_specs.append(c_out)
        out_shape.append(c_shape)
    outs = pl.pallas_call(
        functools.partial(_merge_kernel, pool_width=pool_width, conv_width=conv_width, cast_next=cast_next,
                          n_casts=len(cast_weights)),
        grid=(T // tm,),
        in_specs=in_specs,
        out_specs=out_specs,
        out_shape=out_shape,
        compiler_params=_compiler_params(("parallel",)),
        name="merge",
    )(*args)
    next_w = tuple(outs[2:4]) if cast_next else (None, None)
    return (outs[0], outs[1]) + next_w + tuple(outs[len(outs) - len(cast_weights):])


def _ffn_up_kernel(x_ref, wa_ref, wb_ref, o_ref):
    for r0 in range(0, x_ref.shape[0], GATE_ROWS):
        rows = slice(r0, min(r0 + GATE_ROWS, x_ref.shape[0]))
        xn = x_ref[rows, :]
        a = jnp.dot(xn, wa_ref[...], preferred_element_type=F32)
        b = jnp.dot(xn, wb_ref[...], preferred_element_type=F32)
        o_ref[rows, :] = (a * _sigmoid(a) * b).astype(o_ref.dtype)


def _ffn_up(xn, w_in, *, tm, tn):
    T, D = xn.shape
    hidden = w_in.shape[1] // 2
    nh = hidden // tn
    return pl.pallas_call(
        _ffn_up_kernel,
        grid=(T // tm, nh),
        in_specs=[
            pl.BlockSpec((tm, D), lambda i, j: (i, 0)),
            pl.BlockSpec((D, tn), lambda i, j: (0, j)),
            pl.BlockSpec((D, tn), lambda i, j: (0, j + nh)),
        ],
        out_specs=pl.BlockSpec((tm, tn), lambda i, j: (i, j)),
        out_shape=jax.ShapeDtypeStruct((T, hidden), BF16),
        compiler_params=_compiler_params(("parallel", "arbitrary")),
        name="ffn_up",
    )(xn, w_in, w_in)


def _ffn_down_kernel(h_ref, x_ref, wo_ref, gpost_ref, *rest, emit_next):
    if emit_next:
        gnext_ref, o_ref, xn_out_ref = rest
    else:
        (o_ref,) = rest
    k = pl.program_id(1)

    @pl.when(k == 0)
    def _():
        o_ref[...] = jnp.zeros_like(o_ref)

    o_ref[...] += jnp.dot(h_ref[...], wo_ref[...], preferred_element_type=F32)

    @pl.when(k == pl.num_programs(1) - 1)
    def _():
        x_new = x_ref[...] + _rms_norm(o_ref[...], gpost_ref[...])
        o_ref[...] = x_new
        if emit_next:
            xn_out_ref[...] = _rms_norm(x_new, gnext_ref[...]).astype(xn_out_ref.dtype)


def _ffn_down(hid, x, w_out, g_post, g_next, *, tm, tk):
    T, D = x.shape
    hidden = w_out.shape[0]
    emit_next = g_next is not None

    def rows():
        return pl.BlockSpec((tm, D), lambda i, k: (i, 0))

    def vec():
        return pl.BlockSpec((1, D), lambda i, k: (0, 0))

    in_specs = [pl.BlockSpec((tm, tk), lambda i, k: (i, k)), rows(), pl.BlockSpec((tk, D), lambda i, k: (k, 0)), vec()]
    args = [hid, x, w_out, g_post]
    out_specs = [rows()]
    out_shape = [jax.ShapeDtypeStruct((T, D), F32)]
    if emit_next:
        in_specs.append(vec())
        args.append(g_next)
        out_specs.append(rows())
        out_shape.append(jax.ShapeDtypeStruct((T, D), BF16))
    outs = pl.pallas_call(
        functools.partial(_ffn_down_kernel, emit_next=emit_next),
        grid=(T // tm, hidden // tk),
        in_specs=in_specs,
        out_specs=out_specs,
        out_shape=out_shape,
        compiler_params=_compiler_params(("parallel", "arbitrary")),
        name="ffn_down",
    )(*args)
    return (outs[0], outs[1]) if emit_next else (outs[0], None)


def kernel(x, positions, g_mix_pre, g_mix_post, g_ffn_pre, g_ffn_post, w_in, pool_w, pool_scale, conv_dw,
           conv_b, conv_ln_g, conv_ln_b, ret_gn_g, w_pool_proj, w_conv_proj, w_ret_proj, w_out, w_ffn_in,
           w_ffn_out):
    B, S, D = x.shape
    depth = w_in.shape[0]
    T = B * S
    pool_width = pool_scale.shape[-1]
    conv_width = conv_b.shape[-1]
    ret_width = ret_gn_g.shape[-1]
    n_mix = pool_width + 2 * conv_width + 4 * ret_width
    hidden = w_ffn_out.shape[1]
    head_dim = ret_width // RET_HEADS
    chunk = RET_CHUNK

    tm_mix = _pick(S, (MIX_ROWS,))
    tm_gate = _pick(T, (2048, 1024, 512, 256, 128))
    tn_gate = _pick(N_BRANCH * D, (2048, 1024, 512, 256, 128))
    tm_merge = _pick(T, (256, 128))
    tm_up = _pick(T, (2048, 1024, 512, 256, 128))
    tn_up = _pick(hidden, (512, 256, 128))
    tm_down = _pick(T, (512, 256, 128))
    tk_down = _pick(hidden, (2816, 512, 256, 128))

    xf = x.reshape(T, D)
    xn, cosf, sins, w_mix, w_gate = _prologue(xf, g_mix_pre[0].reshape(1, D), positions, head_dim, w_in, n_mix)
    merge_weights = (w_pool_proj, w_conv_proj, w_ret_proj, w_out)
    ffn_weights = (w_ffn_in, w_ffn_out)
    for l in range(depth):
        br, wp16, wc16, wr16, wo16 = _mixers(
            xn, w_mix, l, cosf, sins, pool_w, pool_scale[l], conv_dw[l], conv_b[l], conv_ln_g[l], conv_ln_b[l],
            ret_gn_g[l], merge_weights, seq_len=S, tm=tm_mix, chunk=chunk)
        gates = _proj_gate(xn, w_gate, tm=tm_gate, tn=tn_gate)
        xf, xn, w_mix, w_gate, wfi16, wfo16 = _merge(
            br, gates, xf, wp16, wc16, wr16, wo16, l, g_mix_post[l].reshape(1, D), g_ffn_pre[l].reshape(1, D),
            w_in, n_mix, ffn_weights, tm=tm_merge)
        g_next = g_mix_pre[l + 1].reshape(1, D) if l + 1 < depth else None
        hid = _ffn_up(xn, wfi16, tm=tm_up, tn=tn_up)
        xf, xn = _ffn_down(hid, xf, wfo16, g_ffn_post[l].reshape(1, D), g_next, tm=tm_down, tk=tk_down)
    return xf.reshape(B, S, D)
```

```python
import functools
import math

import jax
import jax.numpy as jnp
from jax import lax
from jax.experimental import pallas as pl
from jax.experimental.pallas import tpu as pltpu

F32 = jnp.float32
BF16 = jnp.bfloat16

NORM_EPS = 1e-6
LN_EPS = 1e-5
ROPE_BASE = 10000.0
POOL_WINDOWS = (2, 4, 8, 16)
RET_HEADS = 8
N_BRANCH = 3

LANES = 128
SUBLANES = 8
BF16_ROWS = 16
VMEM_LIMIT_BYTES = 56 * 1024 * 1024
MERGE_VMEM_LIMIT_BYTES = 59 * 1024 * 1024

POOL_HALO = 16
CONV_HALO = 32
CONV_ROWS = 64
RET_HEAD_GROUP = 4
GATE_ROWS = 512
MIX_ROWS = 512
RET_CHUNK = 256


def _rms_norm(x, g):
    return x * lax.rsqrt(jnp.mean(x * x, axis=-1, keepdims=True) + NORM_EPS) * g


def _sigmoid(x):
    return 0.5 * jnp.tanh(0.5 * x) + 0.5


def _compiler_params(semantics, vmem_limit_bytes=VMEM_LIMIT_BYTES):
    return pltpu.CompilerParams(dimension_semantics=semantics, vmem_limit_bytes=vmem_limit_bytes)


def _pick(n, candidates):
    for c in candidates:
        if n % c == 0:
            return c
    raise ValueError(f"no tile in {candidates} divides {n}")


def _rope(pos, half):
    lane = lax.broadcasted_iota(jnp.int32, (1, 2 * half), 1)
    freq = (lane % half).astype(F32)
    inv = jnp.exp(freq * (-math.log(ROPE_BASE) / half))
    ang = pos * inv
    sin = jnp.sin(ang)
    return jnp.cos(ang), jnp.where(lane < half, -sin, sin)


def _cast_specs(w, layer, n_steps, period=1):
    _, R, C = w.shape
    rows = R * period // n_steps
    assert R * period % n_steps == 0 and rows % BF16_ROWS == 0 and C % LANES == 0
    in_spec = pl.BlockSpec((None, rows, C), lambda i: (layer, i // period, 0))
    out_spec = pl.BlockSpec((rows, C), lambda i: (i // period, 0))
    return in_spec, out_spec, jax.ShapeDtypeStruct((R, C), BF16)


def _cast_period(w, n_steps):
    period = 1
    while (w.shape[1] * period) % (n_steps * BF16_ROWS):
        period *= 2
    return period


def _split_cast_specs(w, layer, n_steps, n_first):
    _, R, C = w.shape
    rows = R // n_steps
    assert R % n_steps == 0 and rows % BF16_ROWS == 0 and n_first % LANES == 0 and C % LANES == 0
    in_spec = pl.BlockSpec((None, rows, C), lambda i: (layer, i, 0))
    out_specs = [pl.BlockSpec((rows, n_first), lambda i: (i, 0)), pl.BlockSpec((rows, C - n_first), lambda i: (i, 0))]
    out_shapes = [jax.ShapeDtypeStruct((R, n_first), BF16), jax.ShapeDtypeStruct((R, C - n_first), BF16)]
    return in_spec, out_specs, out_shapes


def _grid2_cast_blocks(w, n_i, n_j):
    R = w.shape[1]
    for m in range(n_j, 0, -1):
        if R % (n_i * m) == 0 and (R // (n_i * m)) % BF16_ROWS == 0:
            return m, R // (n_i * m)
    raise ValueError(f"cannot split {R} rows over a ({n_i}, {n_j}) grid")


def _grid2_cast_specs(w, layer, n_i, n_j, n_first=None):
    _, R, C = w.shape
    m, rows = _grid2_cast_blocks(w, n_i, n_j)

    def block(i, j):
        return i * m + jnp.minimum(j, m - 1)

    in_spec = pl.BlockSpec((None, rows, C), lambda i, j: (layer, block(i, j), 0))
    widths = [C] if n_first is None else [n_first, C - n_first]
    assert all(width % LANES == 0 for width in widths)
    out_specs = [pl.BlockSpec((rows, width), lambda i, j: (block(i, j), 0)) for width in widths]
    out_shapes = [jax.ShapeDtypeStruct((R, width), BF16) for width in widths]
    return in_spec, out_specs, out_shapes


def _split_cast(w_ref, first_ref, second_ref):
    n_first = first_ref.shape[1]
    first_ref[...] = w_ref[:, 0:n_first].astype(first_ref.dtype)
    second_ref[...] = w_ref[:, n_first:].astype(second_ref.dtype)


def _prologue_kernel(x_ref, g_ref, pos_ref, w_ref, wfi_ref, o_ref, cos_ref, sin_ref, wmix_ref, wgate_ref,
                     wfi16_ref):
    o_ref[...] = _rms_norm(x_ref[...], g_ref[...]).astype(o_ref.dtype)
    cos_ref[...], sin_ref[...] = _rope(pos_ref[...], pos_ref.shape[1] // 2)
    _split_cast(w_ref, wmix_ref, wgate_ref)
    wfi16_ref[...] = wfi_ref[...].astype(wfi16_ref.dtype)


def _prologue(x, g, positions, head_dim, w_in, n_mix, w_ffn_in):
    T, D = x.shape
    tm = _pick(T, (512, 256, 128))
    posb = jnp.broadcast_to(positions.astype(F32).reshape(T, 1), (T, head_dim))
    w_spec, w_out_specs, w_out_shapes = _split_cast_specs(w_in, 0, T // tm, n_mix)
    wfi_spec, wfi_out_spec, wfi_shape = _cast_specs(w_ffn_in, 0, T // tm, _cast_period(w_ffn_in, T // tm))
    rows = pl.BlockSpec((tm, D), lambda i: (i, 0))
    rope = pl.BlockSpec((tm, head_dim), lambda i: (i, 0))
    return pl.pallas_call(
        _prologue_kernel,
        grid=(T // tm,),
        in_specs=[rows, pl.BlockSpec((1, D), lambda i: (0, 0)), rope, w_spec, wfi_spec],
        out_specs=[rows, rope, rope] + w_out_specs + [wfi_out_spec],
        out_shape=[jax.ShapeDtypeStruct((T, D), BF16)] + [jax.ShapeDtypeStruct((T, head_dim), F32)] * 2
        + w_out_shapes + [wfi_shape],
        compiler_params=_compiler_params(("parallel",)),
        name="prologue",
    )(x, g, posb, w_in, w_ffn_in)


def _proj_gate_kernel(x_ref, w_ref, o_ref):
    for r0 in range(0, x_ref.shape[0], GATE_ROWS):
        rows = slice(r0, min(r0 + GATE_ROWS, x_ref.shape[0]))
        acc = jnp.dot(x_ref[rows, :], w_ref[...], preferred_element_type=F32)
        o_ref[rows, :] = _sigmoid(acc).astype(o_ref.dtype)


def _proj_gate(xn, w, *, tm, tn):
    T, D = xn.shape
    N = w.shape[1]
    return pl.pallas_call(
        _proj_gate_kernel,
        grid=(T // tm, N // tn),
        in_specs=[
            pl.BlockSpec((tm, D), lambda i, j: (i, 0)),
            pl.BlockSpec((D, tn), lambda i, j: (0, j)),
        ],
        out_specs=pl.BlockSpec((tm, tn), lambda i, j: (i, j)),
        out_shape=jax.ShapeDtypeStruct((T, N), BF16),
        compiler_params=_compiler_params(("parallel", "arbitrary")),
        name="proj_gate",
    )(xn, w)


def _rotary(t, cos, sin):
    return t * cos + pltpu.roll(t, t.shape[1] // 2, axis=1) * sin


def _mixer_kernel(x_ref, w_ref, cos_ref, sin_ref, poolw_ref, pscale_ref, dw_ref, cb_ref, lng_ref, lnb_ref,
                  gng_ref, *rest, n_casts, tm, chunk, tiles_per_seq, pool_width, conv_width, conv_kernel,
                  ret_width):
    cast_in, out_ref, cast_out = rest[:n_casts], rest[n_casts], rest[n_casts + 1:2 * n_casts + 1]
    u_scr, h_scr, state_scr, decay_scr, xi_scr, zeta_scr = rest[2 * n_casts + 1:]
    for src_ref, dst_ref in zip(cast_in, cast_out):
        dst_ref[...] = src_ref[...].astype(dst_ref.dtype)

    tile = pl.program_id(0)
    s = tile % tiles_per_seq
    heads = RET_HEADS
    dh = ret_width // heads
    pool_group = pool_width // len(POOL_WINDOWS)
    log_gamma = [math.log1p(-(2.0 ** (-5.0 - h))) for h in range(heads)]

    off_a = pool_width
    off_g = off_a + conv_width
    off_q = off_g + conv_width
    off_k = off_q + ret_width
    off_v = off_k + ret_width
    off_gate = off_v + ret_width

    @pl.when(tile == 0)
    def _():
        row = lax.broadcasted_iota(jnp.int32, (chunk, chunk), 0).astype(F32)
        col = lax.broadcasted_iota(jnp.int32, (chunk, chunk), 1).astype(F32)
        rel = row - col
        pos = lax.broadcasted_iota(jnp.int32, (chunk, dh), 0).astype(F32)
        for h in range(heads):
            decay_scr[h] = jnp.where(rel >= 0, jnp.exp(jnp.maximum(rel, 0.0) * log_gamma[h]), 0.0)
            xi_scr[h] = jnp.exp((pos + 1.0) * log_gamma[h])
            zeta_scr[h] = jnp.exp((chunk - 1.0 - pos) * log_gamma[h])

    @pl.when(s == 0)
    def _():
        u_scr[0:POOL_HALO, :] = jnp.zeros((POOL_HALO, pool_width), F32)
        h_scr[0:CONV_HALO, :] = jnp.zeros((CONV_HALO, conv_width), F32)
        state_scr[...] = jnp.zeros_like(state_scr)

    for r_off in range(0, tm, MIX_ROWS):
        sub = slice(r_off, r_off + MIX_ROWS)

        def stream(c0, width, sub=sub):
            return jnp.dot(x_ref[sub, :], w_ref[:, c0:c0 + width], preferred_element_type=F32)

        u_scr[POOL_HALO + r_off:POOL_HALO + r_off + MIX_ROWS, :] = stream(0, pool_width)
        t_idx = s * tm + r_off + lax.broadcasted_iota(jnp.int32, (MIX_ROWS, 1), 0)
        for gi, w in enumerate(POOL_WINDOWS):
            c0 = gi * pool_group
            ug = u_scr[r_off:r_off + POOL_HALO + MIX_ROWS, c0:c0 + pool_group]
            win = ug
            span = 1
            while span < w:
                win = win + pltpu.roll(win, span, axis=0)
                span *= 2
            cnt = jnp.minimum(t_idx + 1, w).astype(F32)
            p = win[POOL_HALO:, :] / cnt - ug[POOL_HALO:, :]
            y = jnp.dot(p.astype(BF16), poolw_ref[gi].astype(BF16), preferred_element_type=F32)
            out_ref[sub, c0:c0 + pool_group] = (y * pscale_ref[:, c0:c0 + pool_group]).astype(out_ref.dtype)

        h_scr[CONV_HALO + r_off:CONV_HALO + r_off + MIX_ROWS, :] = (
            stream(off_a, conv_width) * _sigmoid(stream(off_g, conv_width)))
        win_rows = CONV_ROWS + SUBLANES
        for r0 in range(r_off, r_off + MIX_ROWS, CONV_ROWS):
            base = CONV_HALO + r0 - SUBLANES
            blocks = []
            for c0 in range(0, conv_width, LANES):
                cols = slice(c0, c0 + LANES)
                acc = None
                for r in range(SUBLANES):
                    part = None
                    for a in range((conv_kernel - 1 - r) // SUBLANES + 1):
                        j = conv_kernel - 1 - (SUBLANES * a + r)
                        lo = base - SUBLANES * a
                        term = dw_ref[j:j + 1, cols] * h_scr[lo:lo + win_rows, cols]
                        part = term if part is None else part + term
                    if r:
                        part = pltpu.roll(part, r, axis=0)
                    acc = part if acc is None else acc + part
                blocks.append(acc[SUBLANES:, :] + cb_ref[:, cols])
            conv = jnp.concatenate(blocks, axis=1)
            mu = jnp.mean(conv, axis=-1, keepdims=True)
            d = conv - mu
            var = jnp.mean(d * d, axis=-1, keepdims=True)
            y = d * lax.rsqrt(var + LN_EPS) * lng_ref[...] + lnb_ref[...]
            out_ref[r0:r0 + CONV_ROWS, pool_width:pool_width + conv_width] = (
                y * _sigmoid(y)).astype(out_ref.dtype)

        out_off = pool_width + conv_width
        cosf = cos_ref[sub, :]
        sins = sin_ref[sub, :]
        group = RET_HEAD_GROUP
        for h0 in range(0, heads, group):
            width = group * dh
            q_g = stream(off_q + h0 * dh, width)
            k_g = stream(off_k + h0 * dh, width)
            v_g = stream(off_v + h0 * dh, width)
            gate_g = stream(off_gate + h0 * dh, width)
            for hh in range(group):
                h = h0 + hh
                lanes = slice(hh * dh, (hh + 1) * dh)
                gamma_c = math.exp(chunk * log_gamma[h])
                qr = _rotary(q_g[:, lanes], cosf, sins).astype(BF16)
                kr = (_rotary(k_g[:, lanes], cosf, sins) * (dh ** -0.5)).astype(BF16)
                for c in range(MIX_ROWS // chunk):
                    rows = slice(c * chunk, (c + 1) * chunk)
                    q = qr[rows]
                    k = kr[rows]
                    v = v_g[rows, lanes]
                    scores = lax.dot_general(q, k, (((1,), (1,)), ((), ())), preferred_element_type=F32)
                    scores = scores * decay_scr[h]
                    intra = jnp.dot(scores.astype(BF16), v.astype(BF16), preferred_element_type=F32)
                    state = state_scr[h]
                    inter = jnp.dot(q, state.astype(BF16), preferred_element_type=F32) * xi_scr[h]
                    kv = lax.dot_general(k, (v * zeta_scr[h]).astype(BF16), (((0,), (0,)), ((), ())),
                                         preferred_element_type=F32)
                    state_scr[h] = gamma_c * state + kv
                    o = intra + inter
                    mu = jnp.mean(o, axis=-1, keepdims=True)
                    d = o - mu
                    var = jnp.mean(d * d, axis=-1, keepdims=True)
                    o = d * lax.rsqrt(var + LN_EPS) * gng_ref[:, h * dh:(h + 1) * dh]
                    gate = gate_g[rows, lanes]
                    out_rows = slice(r_off + c * chunk, r_off + (c + 1) * chunk)
                    out_ref[out_rows, out_off + h * dh:out_off + (h + 1) * dh] = (
                        gate * _sigmoid(gate) * o).astype(out_ref.dtype)

    u_scr[0:POOL_HALO, :] = u_scr[tm:tm + POOL_HALO, :]
    h_scr[0:CONV_HALO, :] = h_scr[tm:tm + CONV_HALO, :]


def _mixers(xn, w_mix, layer, cosf, sins, pool_w, pool_scale, conv_dw, conv_b, ln_g, ln_b, gn_g, cast_weights,
            *, seq_len, tm, chunk):
    T, d_model = xn.shape
    pool_width = pool_scale.shape[-1]
    conv_kernel, conv_width = conv_dw.shape
    ret_width = gn_g.shape[-1]
    n_mix = pool_width + 2 * conv_width + 4 * ret_width
    dh = ret_width // RET_HEADS
    n_groups = len(POOL_WINDOWS)
    pool_group = pool_width // n_groups
    n_steps = T // tm
    assert pool_width + conv_width + ret_width == d_model and w_mix.shape == (d_model, n_mix)
    assert dh == LANES and pool_group == LANES and conv_width % LANES == 0 and RET_HEADS % RET_HEAD_GROUP == 0
    assert chunk % SUBLANES == 0 and MIX_ROWS % chunk == 0 and MIX_ROWS % CONV_ROWS == 0
    assert tm % MIX_ROWS == 0 and seq_len % tm == 0
    assert max(POOL_WINDOWS) - 1 <= POOL_HALO and SUBLANES * ((conv_kernel - 1) // SUBLANES + 1) <= CONV_HALO

    def row(v):
        return v.reshape(1, -1)

    def const_spec(shape):
        return pl.BlockSpec(shape, lambda i: (0,) * len(shape))

    casts = [_cast_specs(w, layer, n_steps, _cast_period(w, n_steps)) for w in cast_weights]

    kern = functools.partial(_mixer_kernel, n_casts=len(casts), tm=tm, chunk=chunk, tiles_per_seq=seq_len // tm,
                             pool_width=pool_width, conv_width=conv_width, conv_kernel=conv_kernel,
                             ret_width=ret_width)
    return pl.pallas_call(
        kern,
        grid=(n_steps,),
        in_specs=[
            pl.BlockSpec((tm, d_model), lambda i: (i, 0)),
            pl.BlockSpec((d_model, n_mix), lambda i: (0, 0), pipeline_mode=pl.Buffered(1)),
            pl.BlockSpec((tm, dh), lambda i: (i, 0)),
            pl.BlockSpec((tm, dh), lambda i: (i, 0)),
            pl.BlockSpec((None, n_groups, pool_group, pool_group), lambda i: (layer, 0, 0, 0)),
            const_spec((1, pool_width)),
            const_spec((conv_kernel, conv_width)),
            const_spec((1, conv_width)),
            const_spec((1, conv_width)),
            const_spec((1, conv_width)),
            const_spec((1, ret_width)),
        ] + [c[0] for c in casts],
        out_specs=[pl.BlockSpec((tm, d_model), lambda i: (i, 0))] + [c[1] for c in casts],
        out_shape=[jax.ShapeDtypeStruct((T, d_model), BF16)] + [c[2] for c in casts],
        scratch_shapes=[
            pltpu.VMEM((POOL_HALO + tm, pool_width), F32),
            pltpu.VMEM((CONV_HALO + tm, conv_width), F32),
            pltpu.VMEM((RET_HEADS, dh, dh), F32),
            pltpu.VMEM((RET_HEADS, chunk, chunk), F32),
            pltpu.VMEM((RET_HEADS, chunk, dh), F32),
            pltpu.VMEM((RET_HEADS, chunk, dh), F32),
        ],
        compiler_params=_compiler_params(("arbitrary",)),
        name="mixers",
    )(xn, w_mix, cosf, sins, pool_w, row(pool_scale), conv_dw, row(conv_b), row(ln_g), row(ln_b), row(gn_g),
      *cast_weights)


def _merge_kernel(br_ref, gate_ref, x_ref, wp_ref, wc_ref, wr_ref, wo_ref, g_ref, gn_ref, o_ref, xn_ref,
                  *, pool_width, conv_width):
    d = x_ref.shape[-1]
    c1 = pool_width
    c2 = pool_width + conv_width
    y_pool = jnp.dot(br_ref[:, 0:c1], wp_ref[...], preferred_element_type=F32)
    merged = gate_ref[:, 0:d].astype(F32) * y_pool
    y_conv = jnp.dot(br_ref[:, c1:c2], wc_ref[...], preferred_element_type=F32)
    merged = merged + gate_ref[:, d:2 * d].astype(F32) * y_conv
    y_ret = jnp.dot(br_ref[:, c2:d], wr_ref[...], preferred_element_type=F32)
    merged = merged + gate_ref[:, 2 * d:3 * d].astype(F32) * y_ret
    out = jnp.dot(merged.astype(BF16), wo_ref[...], preferred_element_type=F32)
    x_new = x_ref[...] + _rms_norm(out, g_ref[...])
    o_ref[...] = x_new
    xn_ref[...] = _rms_norm(x_new, gn_ref[...]).astype(xn_ref.dtype)


def _merge(br, gates, x, wp, wc, wr, wo, g_post, g_next, *, tm):
    T, D = x.shape
    pool_width, conv_width, ret_width = wp.shape[0], wc.shape[0], wr.shape[0]

    def resident(rows):
        return pl.BlockSpec((rows, D), lambda i: (0, 0), pipeline_mode=pl.Buffered(1))

    def vec():
        return pl.BlockSpec((1, D), lambda i: (0, 0))

    def rows(width):
        return pl.BlockSpec((tm, width), lambda i: (i, 0))

    return pl.pallas_call(
        functools.partial(_merge_kernel, pool_width=pool_width, conv_width=conv_width),
        grid=(T // tm,),
        in_specs=[rows(D), rows(N_BRANCH * D), rows(D), resident(pool_width), resident(conv_width),
                  resident(ret_width), resident(D), vec(), vec()],
        out_specs=[rows(D), rows(D)],
        out_shape=[jax.ShapeDtypeStruct((T, D), F32), jax.ShapeDtypeStruct((T, D), BF16)],
        compiler_params=_compiler_params(("parallel",), MERGE_VMEM_LIMIT_BYTES),
        name="merge",
    )(br, gates, x, wp, wc, wr, wo, g_post, g_next)


def _ffn_up_kernel(x_ref, wa_ref, wb_ref, *rest, cast_next):
    if cast_next:
        wfo_ref, w_next_ref, wfi_next_ref, o_ref, wfo16_ref, wmix_ref, wgate_ref, wfi16_ref = rest
        _split_cast(w_next_ref, wmix_ref, wgate_ref)
        wfi16_ref[...] = wfi_next_ref[...].astype(wfi16_ref.dtype)
    else:
        wfo_ref, o_ref, wfo16_ref = rest
    wfo16_ref[...] = wfo_ref[...].astype(wfo16_ref.dtype)
    for r0 in range(0, x_ref.shape[0], GATE_ROWS):
        rows = slice(r0, min(r0 + GATE_ROWS, x_ref.shape[0]))
        xn = x_ref[rows, :]
        a = jnp.dot(xn, wa_ref[...], preferred_element_type=F32)
        b = jnp.dot(xn, wb_ref[...], preferred_element_type=F32)
        o_ref[rows, :] = (a * _sigmoid(a) * b).astype(o_ref.dtype)


def _ffn_up(xn, wfi16, layer, w_ffn_out, w_in, n_mix, w_ffn_in, *, tm, tn):
    T, D = xn.shape
    hidden = wfi16.shape[1] // 2
    nh = hidden // tn
    n_i = T // tm
    cast_next = layer + 1 < w_in.shape[0]
    in_specs = [
        pl.BlockSpec((tm, D), lambda i, j: (i, 0)),
        pl.BlockSpec((D, tn), lambda i, j: (0, j)),
        pl.BlockSpec((D, tn), lambda i, j: (0, j + nh)),
    ]
    args = [xn, wfi16, wfi16]
    out_specs = [pl.BlockSpec((tm, tn), lambda i, j: (i, j))]
    out_shape = [jax.ShapeDtypeStruct((T, hidden), BF16)]
    casts = [(w_ffn_out, layer, None)]
    if cast_next:
        casts += [(w_in, layer + 1, n_mix), (w_ffn_in, layer + 1, None)]
    for w, lyr, n_first in casts:
        c_in, c_outs, c_shapes = _grid2_cast_specs(w, lyr, n_i, nh, n_first)
        in_specs.append(c_in)
        args.append(w)
        out_specs += c_outs
        out_shape += c_shapes
    outs = pl.pallas_call(
        functools.partial(_ffn_up_kernel, cast_next=cast_next),
        grid=(n_i, nh),
        in_specs=in_specs,
        out_specs=out_specs,
        out_shape=out_shape,
        compiler_params=_compiler_params(("arbitrary", "arbitrary")),
        name="ffn_up",
    )(*args)
    return tuple(outs) if cast_next else (outs[0], outs[1], None, None, None)


def _ffn_down_kernel(h_ref, x_ref, wo_ref, gpost_ref, *rest, emit_next):
    if emit_next:
        gnext_ref, o_ref, xn_out_ref = rest
    else:
        (o_ref,) = rest
    k = pl.program_id(1)

    @pl.when(k == 0)
    def _():
        o_ref[...] = jnp.zeros_like(o_ref)

    o_ref[...] += jnp.dot(h_ref[...], wo_ref[...], preferred_element_type=F32)

    @pl.when(k == pl.num_programs(1) - 1)
    def _():
        x_new = x_ref[...] + _rms_norm(o_ref[...], gpost_ref[...])
        o_ref[...] = x_new
        if emit_next:
            xn_out_ref[...] = _rms_norm(x_new, gnext_ref[...]).astype(xn_out_ref.dtype)


def _ffn_down(hid, x, w_out, g_post, g_next, *, tm, tk):
    T, D = x.shape
    hidden = w_out.shape[0]
    emit_next = g_next is not None

    def rows():
        return pl.BlockSpec((tm, D), lambda i, k: (i, 0))

    def vec():
        return pl.BlockSpec((1, D), lambda i, k: (0, 0))

    in_specs = [pl.BlockSpec((tm, tk), lambda i, k: (i, k)), rows(), pl.BlockSpec((tk, D), lambda i, k: (k, 0)), vec()]
    args = [hid, x, w_out, g_post]
    out_specs = [rows()]
    out_shape = [jax.ShapeDtypeStruct((T, D), F32)]
    if emit_next:
        in_specs.append(vec())
        args.append(g_next)
        out_specs.append(rows())
        out_shape.append(jax.ShapeDtypeStruct((T, D), BF16))
    outs = pl.pallas_call(
        functools.partial(_ffn_down_kernel, emit_next=emit_next),
        grid=(T // tm, hidden // tk),
        in_specs=in_specs,
        out_specs=out_specs,
        out_shape=out_shape,
        compiler_params=_compiler_params(("parallel", "arbitrary")),
        name="ffn_down",
    )(*args)
    return (outs[0], outs[1]) if emit_next else (outs[0], None)


def kernel(x, positions, g_mix_pre, g_mix_post, g_ffn_pre, g_ffn_post, w_in, pool_w, pool_scale, conv_dw,
           conv_b, conv_ln_g, conv_ln_b, ret_gn_g, w_pool_proj, w_conv_proj, w_ret_proj, w_out, w_ffn_in,
           w_ffn_out):
    B, S, D = x.shape
    depth = w_in.shape[0]
    T = B * S
    pool_width = pool_scale.shape[-1]
    conv_width = conv_b.shape[-1]
    ret_width = ret_gn_g.shape[-1]
    n_mix = pool_width + 2 * conv_width + 4 * ret_width
    hidden = w_ffn_out.shape[1]
    head_dim = ret_width // RET_HEADS
    chunk = RET_CHUNK

    tm_mix = _pick(S, (MIX_ROWS,))
    tm_gate = _pick(T, (2048, 1024, 512, 256, 128))
    tn_gate = _pick(N_BRANCH * D, (2048, 1024, 512, 256, 128))
    tm_merge = _pick(T, (512, 256, 128))
    tm_up = _pick(T, (2048, 1024, 512, 256, 128))
    tn_up = _pick(hidden, (512, 256, 128))
    tm_down = _pick(T, (512, 256, 128))
    tk_down = _pick(hidden, (2816, 512, 256, 128))

    xf = x.reshape(T, D)
    xn, cosf, sins, w_mix, w_gate, wfi16 = _prologue(xf, g_mix_pre[0].reshape(1, D), positions, head_dim, w_in,
                                                     n_mix, w_ffn_in)
    merge_weights = (w_pool_proj, w_conv_proj, w_ret_proj, w_out)
    for l in range(depth):
        br, wp16, wc16, wr16, wo16 = _mixers(
            xn, w_mix, l, cosf, sins, pool_w, pool_scale[l], conv_dw[l], conv_b[l], conv_ln_g[l], conv_ln_b[l],
            ret_gn_g[l], merge_weights, seq_len=S, tm=tm_mix, chunk=chunk)
        gates = _proj_gate(xn, w_gate, tm=tm_gate, tn=tn_gate)
        xf, xn = _merge(br, gates, xf, wp16, wc16, wr16, wo16, g_mix_post[l].reshape(1, D),
                        g_ffn_pre[l].reshape(1, D), tm=tm_merge)
        hid, wfo16, w_mix, w_gate, wfi16 = _ffn_up(xn, wfi16, l, w_ffn_out, w_in, n_mix, w_ffn_in,
                                                   tm=tm_up, tn=tn_up)
        g_next = g_mix_pre[l + 1].reshape(1, D) if l + 1 < depth else None
        xf, xn = _ffn_down(hid, xf, wfo16, g_ffn_post[l].reshape(1, D), g_next, tm=tm_down, tk=tk_down)
    return xf.reshape(B, S, D)
```

```python
import functools
import math

import jax
import jax.numpy as jnp
from jax import lax
from jax.experimental import pallas as pl
from jax.experimental.pallas import tpu as pltpu

F32 = jnp.float32
BF16 = jnp.bfloat16

NORM_EPS = 1e-6
LN_EPS = 1e-5
ROPE_BASE = 10000.0
POOL_WINDOWS = (2, 4, 8, 16)
RET_HEADS = 8
N_BRANCH = 3

LANES = 128
SUBLANES = 8
BF16_ROWS = 16
VMEM_LIMIT_BYTES = 56 * 1024 * 1024
MERGE_VMEM_LIMIT_BYTES = 59 * 1024 * 1024
FFN_DOWN_VMEM_LIMIT_BYTES = 60 * 1024 * 1024

POOL_HALO = 16
CONV_HALO = 32
CONV_ROWS = 64
RET_HEAD_GROUP = 4
GATE_ROWS = 512
MIX_ROWS = 512
RET_CHUNK = 256


def _rms_norm(x, g):
    return x * lax.rsqrt(jnp.mean(x * x, axis=-1, keepdims=True) + NORM_EPS) * g


def _sigmoid(x):
    return 0.5 * jnp.tanh(0.5 * x) + 0.5


def _compiler_params(semantics, vmem_limit_bytes=VMEM_LIMIT_BYTES):
    return pltpu.CompilerParams(dimension_semantics=semantics, vmem_limit_bytes=vmem_limit_bytes)


def _pick(n, candidates):
    for c in candidates:
        if n % c == 0:
            return c
    raise ValueError(f"no tile in {candidates} divides {n}")


def _rope(pos, half):
    lane = lax.broadcasted_iota(jnp.int32, (1, 2 * half), 1)
    freq = (lane % half).astype(F32)
    inv = jnp.exp(freq * (-math.log(ROPE_BASE) / half))
    ang = pos * inv
    sin = jnp.sin(ang)
    return jnp.cos(ang), jnp.where(lane < half, -sin, sin)


def _cast_specs(w, layer, n_steps, period=1):
    _, R, C = w.shape
    rows = R * period // n_steps
    assert R * period % n_steps == 0 and rows % BF16_ROWS == 0 and C % LANES == 0
    in_spec = pl.BlockSpec((None, rows, C), lambda i: (layer, i // period, 0))
    out_spec = pl.BlockSpec((rows, C), lambda i: (i // period, 0))
    return in_spec, out_spec, jax.ShapeDtypeStruct((R, C), BF16)


def _cast_period(w, n_steps):
    period = 1
    while (w.shape[1] * period) % (n_steps * BF16_ROWS):
        period *= 2
    return period


def _split_cast_specs(w, layer, n_steps, n_first):
    _, R, C = w.shape
    rows = R // n_steps
    assert R % n_steps == 0 and rows % BF16_ROWS == 0 and n_first % LANES == 0 and C % LANES == 0
    in_spec = pl.BlockSpec((None, rows, C), lambda i: (layer, i, 0))
    out_specs = [pl.BlockSpec((rows, n_first), lambda i: (i, 0)), pl.BlockSpec((rows, C - n_first), lambda i: (i, 0))]
    out_shapes = [jax.ShapeDtypeStruct((R, n_first), BF16), jax.ShapeDtypeStruct((R, C - n_first), BF16)]
    return in_spec, out_specs, out_shapes


def _grid2_cast_blocks(w, n_i, n_j):
    R = w.shape[1]
    for m in range(n_j, 0, -1):
        if R % (n_i * m) == 0 and (R // (n_i * m)) % BF16_ROWS == 0:
            return m, R // (n_i * m)
    raise ValueError(f"cannot split {R} rows over a ({n_i}, {n_j}) grid")


def _grid2_cast_specs(w, layer, n_i, n_j, n_first=None):
    _, R, C = w.shape
    m, rows = _grid2_cast_blocks(w, n_i, n_j)

    def block(i, j):
        return i * m + jnp.minimum(j, m - 1)

    in_spec = pl.BlockSpec((None, rows, C), lambda i, j: (layer, block(i, j), 0))
    widths = [C] if n_first is None else [n_first, C - n_first]
    assert all(width % LANES == 0 for width in widths)
    out_specs = [pl.BlockSpec((rows, width), lambda i, j: (block(i, j), 0)) for width in widths]
    out_shapes = [jax.ShapeDtypeStruct((R, width), BF16) for width in widths]
    return in_spec, out_specs, out_shapes


def _split_cast(w_ref, first_ref, second_ref):
    n_first = first_ref.shape[1]
    first_ref[...] = w_ref[:, 0:n_first].astype(first_ref.dtype)
    second_ref[...] = w_ref[:, n_first:].astype(second_ref.dtype)


def _prologue_kernel(x_ref, g_ref, pos_ref, w_ref, wfi_ref, o_ref, cos_ref, sin_ref, wmix_ref, wgate_ref,
                     wfi16_ref):
    o_ref[...] = _rms_norm(x_ref[...], g_ref[...]).astype(o_ref.dtype)
    cos_ref[...], sin_ref[...] = _rope(pos_ref[...], pos_ref.shape[1] // 2)
    _split_cast(w_ref, wmix_ref, wgate_ref)
    wfi16_ref[...] = wfi_ref[...].astype(wfi16_ref.dtype)


def _prologue(x, g, positions, head_dim, w_in, n_mix, w_ffn_in):
    T, D = x.shape
    tm = _pick(T, (512, 256, 128))
    posb = jnp.broadcast_to(positions.astype(F32).reshape(T, 1), (T, head_dim))
    w_spec, w_out_specs, w_out_shapes = _split_cast_specs(w_in, 0, T // tm, n_mix)
    wfi_spec, wfi_out_spec, wfi_shape = _cast_specs(w_ffn_in, 0, T // tm, _cast_period(w_ffn_in, T // tm))
    rows = pl.BlockSpec((tm, D), lambda i: (i, 0))
    rope = pl.BlockSpec((tm, head_dim), lambda i: (i, 0))
    return pl.pallas_call(
        _prologue_kernel,
        grid=(T // tm,),
        in_specs=[rows, pl.BlockSpec((1, D), lambda i: (0, 0)), rope, w_spec, wfi_spec],
        out_specs=[rows, rope, rope] + w_out_specs + [wfi_out_spec],
        out_shape=[jax.ShapeDtypeStruct((T, D), BF16)] + [jax.ShapeDtypeStruct((T, head_dim), F32)] * 2
        + w_out_shapes + [wfi_shape],
        compiler_params=_compiler_params(("parallel",)),
        name="prologue",
    )(x, g, posb, w_in, w_ffn_in)


def _proj_gate_kernel(x_ref, w_ref, o_ref):
    for r0 in range(0, x_ref.shape[0], GATE_ROWS):
        rows = slice(r0, min(r0 + GATE_ROWS, x_ref.shape[0]))
        acc = jnp.dot(x_ref[rows, :], w_ref[...], preferred_element_type=F32)
        o_ref[rows, :] = _sigmoid(acc).astype(o_ref.dtype)


def _proj_gate(xn, w, *, tm, tn):
    T, D = xn.shape
    N = w.shape[1]
    return pl.pallas_call(
        _proj_gate_kernel,
        grid=(T // tm, N // tn),
        in_specs=[
            pl.BlockSpec((tm, D), lambda i, j: (i, 0)),
            pl.BlockSpec((D, tn), lambda i, j: (0, j)),
        ],
        out_specs=pl.BlockSpec((tm, tn), lambda i, j: (i, j)),
        out_shape=jax.ShapeDtypeStruct((T, N), BF16),
        compiler_params=_compiler_params(("parallel", "arbitrary")),
        name="proj_gate",
    )(xn, w)


def _rotary(t, cos, sin):
    return t * cos + pltpu.roll(t, t.shape[1] // 2, axis=1) * sin


def _mixer_kernel(x_ref, w_ref, cos_ref, sin_ref, poolw_ref, pscale_ref, dw_ref, cb_ref, lng_ref, lnb_ref,
                  gng_ref, *rest, n_casts, tm, chunk, tiles_per_seq, pool_width, conv_width, conv_kernel,
                  ret_width):
    cast_in, out_ref, cast_out = rest[:n_casts], rest[n_casts], rest[n_casts + 1:2 * n_casts + 1]
    u_scr, h_scr, state_scr, decay_scr, xi_scr, zeta_scr = rest[2 * n_casts + 1:]
    for src_ref, dst_ref in zip(cast_in, cast_out):
        dst_ref[...] = src_ref[...].astype(dst_ref.dtype)

    tile = pl.program_id(0)
    s = tile % tiles_per_seq
    heads = RET_HEADS
    dh = ret_width // heads
    pool_group = pool_width // len(POOL_WINDOWS)
    log_gamma = [math.log1p(-(2.0 ** (-5.0 - h))) for h in range(heads)]

    off_a = pool_width
    off_g = off_a + conv_width
    off_q = off_g + conv_width
    off_k = off_q + ret_width
    off_v = off_k + ret_width
    off_gate = off_v + ret_width

    @pl.when(tile == 0)
    def _():
        row = lax.broadcasted_iota(jnp.int32, (chunk, chunk), 0).astype(F32)
        col = lax.broadcasted_iota(jnp.int32, (chunk, chunk), 1).astype(F32)
        rel = row - col
        pos = lax.broadcasted_iota(jnp.int32, (chunk, dh), 0).astype(F32)
        for h in range(heads):
            decay_scr[h] = jnp.where(rel >= 0, jnp.exp(jnp.maximum(rel, 0.0) * log_gamma[h]), 0.0)
            xi_scr[h] = jnp.exp((pos + 1.0) * log_gamma[h])
            zeta_scr[h] = jnp.exp((chunk - 1.0 - pos) * log_gamma[h])

    @pl.when(s == 0)
    def _():
        u_scr[0:POOL_HALO, :] = jnp.zeros((POOL_HALO, pool_width), F32)
        h_scr[0:CONV_HALO, :] = jnp.zeros((CONV_HALO, conv_width), F32)
        state_scr[...] = jnp.zeros_like(state_scr)

    for r_off in range(0, tm, MIX_ROWS):
        sub = slice(r_off, r_off + MIX_ROWS)

        def stream(c0, width, sub=sub):
            return jnp.dot(x_ref[sub, :], w_ref[:, c0:c0 + width], preferred_element_type=F32)

        u_scr[POOL_HALO + r_off:POOL_HALO + r_off + MIX_ROWS, :] = stream(0, pool_width)
        t_idx = s * tm + r_off + lax.broadcasted_iota(jnp.int32, (MIX_ROWS, 1), 0)
        for gi, w in enumerate(POOL_WINDOWS):
            c0 = gi * pool_group
            ug = u_scr[r_off:r_off + POOL_HALO + MIX_ROWS, c0:c0 + pool_group]
            win = ug
            span = 1
            while span < w:
                win = win + pltpu.roll(win, span, axis=0)
                span *= 2
            cnt = jnp.minimum(t_idx + 1, w).astype(F32)
            p = win[POOL_HALO:, :] / cnt - ug[POOL_HALO:, :]
            y = jnp.dot(p.astype(BF16), poolw_ref[gi].astype(BF16), preferred_element_type=F32)
            out_ref[sub, c0:c0 + pool_group] = (y * pscale_ref[:, c0:c0 + pool_group]).astype(out_ref.dtype)

        h_scr[CONV_HALO + r_off:CONV_HALO + r_off + MIX_ROWS, :] = (
            stream(off_a, conv_width) * _sigmoid(stream(off_g, conv_width)))
        win_rows = CONV_ROWS + SUBLANES
        for r0 in range(r_off, r_off + MIX_ROWS, CONV_ROWS):
            base = CONV_HALO + r0 - SUBLANES
            blocks = []
            for c0 in range(0, conv_width, LANES):
                cols = slice(c0, c0 + LANES)
                acc = None
                for r in range(SUBLANES):
                    part = None
                    for a in range((conv_kernel - 1 - r) // SUBLANES + 1):
                        j = conv_kernel - 1 - (SUBLANES * a + r)
                        lo = base - SUBLANES * a
                        term = dw_ref[j:j + 1, cols] * h_scr[lo:lo + win_rows, cols]
                        part = term if part is None else part + term
                    if r:
                        part = pltpu.roll(part, r, axis=0)
                    acc = part if acc is None else acc + part
                blocks.append(acc[SUBLANES:, :] + cb_ref[:, cols])
            conv = jnp.concatenate(blocks, axis=1)
            mu = jnp.mean(conv, axis=-1, keepdims=True)
            d = conv - mu
            var = jnp.mean(d * d, axis=-1, keepdims=True)
            y = d * lax.rsqrt(var + LN_EPS) * lng_ref[...] + lnb_ref[...]
            out_ref[r0:r0 + CONV_ROWS, pool_width:pool_width + conv_width] = (
                y * _sigmoid(y)).astype(out_ref.dtype)

        out_off = pool_width + conv_width
        cosf = cos_ref[sub, :]
        sins = sin_ref[sub, :]
        group = RET_HEAD_GROUP
        for h0 in range(0, heads, group):
            width = group * dh
            q_g = stream(off_q + h0 * dh, width)
            k_g = stream(off_k + h0 * dh, width)
            v_g = stream(off_v + h0 * dh, width)
            gate_g = stream(off_gate + h0 * dh, width)
            for hh in range(group):
                h = h0 + hh
                lanes = slice(hh * dh, (hh + 1) * dh)
                gamma_c = math.exp(chunk * log_gamma[h])
                qr = _rotary(q_g[:, lanes], cosf, sins).astype(BF16)
                kr = (_rotary(k_g[:, lanes], cosf, sins) * (dh ** -0.5)).astype(BF16)
                for c in range(MIX_ROWS // chunk):
                    rows = slice(c * chunk, (c + 1) * chunk)
                    q = qr[rows]
                    k = kr[rows]
                    v = v_g[rows, lanes]
                    scores = lax.dot_general(q, k, (((1,), (1,)), ((), ())), preferred_element_type=F32)
                    scores = scores * decay_scr[h]
                    intra = jnp.dot(scores.astype(BF16), v.astype(BF16), preferred_element_type=F32)
                    state = state_scr[h]
                    inter = jnp.dot(q, state.astype(BF16), preferred_element_type=F32) * xi_scr[h]
                    kv = lax.dot_general(k, (v * zeta_scr[h]).astype(BF16), (((0,), (0,)), ((), ())),
                                         preferred_element_type=F32)
                    state_scr[h] = gamma_c * state + kv
                    o = intra + inter
                    mu = jnp.mean(o, axis=-1, keepdims=True)
                    d = o - mu
                    var = jnp.mean(d * d, axis=-1, keepdims=True)
                    o = d * lax.rsqrt(var + LN_EPS) * gng_ref[:, h * dh:(h + 1) * dh]
                    gate = gate_g[rows, lanes]
                    out_rows = slice(r_off + c * chunk, r_off + (c + 1) * chunk)
                    out_ref[out_rows, out_off + h * dh:out_off + (h + 1) * dh] = (
                        gate * _sigmoid(gate) * o).astype(out_ref.dtype)

    u_scr[0:POOL_HALO, :] = u_scr[tm:tm + POOL_HALO, :]
    h_scr[0:CONV_HALO, :] = h_scr[tm:tm + CONV_HALO, :]


def _mixers(xn, w_mix, layer, cosf, sins, pool_w, pool_scale, conv_dw, conv_b, ln_g, ln_b, gn_g, cast_weights,
            *, seq_len, tm, chunk):
    T, d_model = xn.shape
    pool_width = pool_scale.shape[-1]
    conv_kernel, conv_width = conv_dw.shape
    ret_width = gn_g.shape[-1]
    n_mix = pool_width + 2 * conv_width + 4 * ret_width
    dh = ret_width // RET_HEADS
    n_groups = len(POOL_WINDOWS)
    pool_group = pool_width // n_groups
    n_steps = T // tm
    assert pool_width + conv_width + ret_width == d_model and w_mix.shape == (d_model, n_mix)
    assert dh == LANES and pool_group == LANES and conv_width % LANES == 0 and RET_HEADS % RET_HEAD_GROUP == 0
    assert chunk % SUBLANES == 0 and MIX_ROWS % chunk == 0 and MIX_ROWS % CONV_ROWS == 0
    assert tm % MIX_ROWS == 0 and seq_len % tm == 0
    assert max(POOL_WINDOWS) - 1 <= POOL_HALO and SUBLANES * ((conv_kernel - 1) // SUBLANES + 1) <= CONV_HALO

    def row(v):
        return v.reshape(1, -1)

    def const_spec(shape):
        return pl.BlockSpec(shape, lambda i: (0,) * len(shape))

    casts = [_cast_specs(w, layer, n_steps, _cast_period(w, n_steps)) for w in cast_weights]

    kern = functools.partial(_mixer_kernel, n_casts=len(casts), tm=tm, chunk=chunk, tiles_per_seq=seq_len // tm,
                             pool_width=pool_width, conv_width=conv_width, conv_kernel=conv_kernel,
                             ret_width=ret_width)
    return pl.pallas_call(
        kern,
        grid=(n_steps,),
        in_specs=[
            pl.BlockSpec((tm, d_model), lambda i: (i, 0)),
            pl.BlockSpec((d_model, n_mix), lambda i: (0, 0), pipeline_mode=pl.Buffered(1)),
            pl.BlockSpec((tm, dh), lambda i: (i, 0)),
            pl.BlockSpec((tm, dh), lambda i: (i, 0)),
            pl.BlockSpec((None, n_groups, pool_group, pool_group), lambda i: (layer, 0, 0, 0)),
            const_spec((1, pool_width)),
            const_spec((conv_kernel, conv_width)),
            const_spec((1, conv_width)),
            const_spec((1, conv_width)),
            const_spec((1, conv_width)),
            const_spec((1, ret_width)),
        ] + [c[0] for c in casts],
        out_specs=[pl.BlockSpec((tm, d_model), lambda i: (i, 0))] + [c[1] for c in casts],
        out_shape=[jax.ShapeDtypeStruct((T, d_model), BF16)] + [c[2] for c in casts],
        scratch_shapes=[
            pltpu.VMEM((POOL_HALO + tm, pool_width), F32),
            pltpu.VMEM((CONV_HALO + tm, conv_width), F32),
            pltpu.VMEM((RET_HEADS, dh, dh), F32),
            pltpu.VMEM((RET_HEADS, chunk, chunk), F32),
            pltpu.VMEM((RET_HEADS, chunk, dh), F32),
            pltpu.VMEM((RET_HEADS, chunk, dh), F32),
        ],
        compiler_params=_compiler_params(("arbitrary",)),
        name="mixers",
    )(xn, w_mix, cosf, sins, pool_w, row(pool_scale), conv_dw, row(conv_b), row(ln_g), row(ln_b), row(gn_g),
      *cast_weights)


def _merge_kernel(br_ref, gate_ref, x_ref, wp_ref, wc_ref, wr_ref, wo_ref, g_ref, gn_ref, o_ref, xn_ref,
                  *, pool_width, conv_width):
    d = x_ref.shape[-1]
    c1 = pool_width
    c2 = pool_width + conv_width
    y_pool = jnp.dot(br_ref[:, 0:c1], wp_ref[...], preferred_element_type=F32)
    merged = gate_ref[:, 0:d].astype(F32) * y_pool
    y_conv = jnp.dot(br_ref[:, c1:c2], wc_ref[...], preferred_element_type=F32)
    merged = merged + gate_ref[:, d:2 * d].astype(F32) * y_conv
    y_ret = jnp.dot(br_ref[:, c2:d], wr_ref[...], preferred_element_type=F32)
    merged = merged + gate_ref[:, 2 * d:3 * d].astype(F32) * y_ret
    out = jnp.dot(merged.astype(BF16), wo_ref[...], preferred_element_type=F32)
    x_new = x_ref[...] + _rms_norm(out, g_ref[...])
    o_ref[...] = x_new
    xn_ref[...] = _rms_norm(x_new, gn_ref[...]).astype(xn_ref.dtype)


def _merge(br, gates, x, wp, wc, wr, wo, g_post, g_next, *, tm):
    T, D = x.shape
    pool_width, conv_width, ret_width = wp.shape[0], wc.shape[0], wr.shape[0]

    def resident(rows):
        return pl.BlockSpec((rows, D), lambda i: (0, 0), pipeline_mode=pl.Buffered(1))

    def vec():
        return pl.BlockSpec((1, D), lambda i: (0, 0))

    def rows(width):
        return pl.BlockSpec((tm, width), lambda i: (i, 0))

    return pl.pallas_call(
        functools.partial(_merge_kernel, pool_width=pool_width, conv_width=conv_width),
        grid=(T // tm,),
        in_specs=[rows(D), rows(N_BRANCH * D), rows(D), resident(pool_width), resident(conv_width),
                  resident(ret_width), resident(D), vec(), vec()],
        out_specs=[rows(D), rows(D)],
        out_shape=[jax.ShapeDtypeStruct((T, D), F32), jax.ShapeDtypeStruct((T, D), BF16)],
        compiler_params=_compiler_params(("parallel",), MERGE_VMEM_LIMIT_BYTES),
        name="merge",
    )(br, gates, x, wp, wc, wr, wo, g_post, g_next)


def _ffn_up_kernel(x_ref, wa_ref, wb_ref, *rest, cast_next):
    if cast_next:
        wfo_ref, w_next_ref, wfi_next_ref, o_ref, wfo16_ref, wmix_ref, wgate_ref, wfi16_ref = rest
        _split_cast(w_next_ref, wmix_ref, wgate_ref)
        wfi16_ref[...] = wfi_next_ref[...].astype(wfi16_ref.dtype)
    else:
        wfo_ref, o_ref, wfo16_ref = rest
    wfo16_ref[...] = wfo_ref[...].astype(wfo16_ref.dtype)
    for r0 in range(0, x_ref.shape[0], GATE_ROWS):
        rows = slice(r0, min(r0 + GATE_ROWS, x_ref.shape[0]))
        xn = x_ref[rows, :]
        a = jnp.dot(xn, wa_ref[...], preferred_element_type=F32)
        b = jnp.dot(xn, wb_ref[...], preferred_element_type=F32)
        o_ref[rows, :] = (a * _sigmoid(a) * b).astype(o_ref.dtype)


def _ffn_up(xn, wfi16, layer, w_ffn_out, w_in, n_mix, w_ffn_in, *, tm, tn):
    T, D = xn.shape
    hidden = wfi16.shape[1] // 2
    nh = hidden // tn
    n_i = T // tm
    cast_next = layer + 1 < w_in.shape[0]
    in_specs = [
        pl.BlockSpec((tm, D), lambda i, j: (i, 0)),
        pl.BlockSpec((D, tn), lambda i, j: (0, j)),
        pl.BlockSpec((D, tn), lambda i, j: (0, j + nh)),
    ]
    args = [xn, wfi16, wfi16]
    out_specs = [pl.BlockSpec((tm, tn), lambda i, j: (i, j))]
    out_shape = [jax.ShapeDtypeStruct((T, hidden), BF16)]
    casts = [(w_ffn_out, layer, None)]
    if cast_next:
        casts += [(w_in, layer + 1, n_mix), (w_ffn_in, layer + 1, None)]
    for w, lyr, n_first in casts:
        c_in, c_outs, c_shapes = _grid2_cast_specs(w, lyr, n_i, nh, n_first)
        in_specs.append(c_in)
        args.append(w)
        out_specs += c_outs
        out_shape += c_shapes
    outs = pl.pallas_call(
        functools.partial(_ffn_up_kernel, cast_next=cast_next),
        grid=(n_i, nh),
        in_specs=in_specs,
        out_specs=out_specs,
        out_shape=out_shape,
        compiler_params=_compiler_params(("arbitrary", "arbitrary")),
        name="ffn_up",
    )(*args)
    return tuple(outs) if cast_next else (outs[0], outs[1], None, None, None)


def _ffn_down_kernel(h_ref, x_ref, wo_ref, gpost_ref, *rest, emit_next):
    if emit_next:
        gnext_ref, o_ref, xn_out_ref = rest
    else:
        (o_ref,) = rest
    o_ref[...] = jnp.dot(h_ref[...], wo_ref[...], preferred_element_type=F32)
    x_new = x_ref[...] + _rms_norm(o_ref[...], gpost_ref[...])
    o_ref[...] = x_new
    if emit_next:
        xn_out_ref[...] = _rms_norm(x_new, gnext_ref[...]).astype(xn_out_ref.dtype)


def _ffn_down(hid, x, w_out, g_post, g_next, *, tm):
    T, D = x.shape
    hidden = w_out.shape[0]
    emit_next = g_next is not None

    def rows(width):
        return pl.BlockSpec((tm, width), lambda i: (i, 0))

    def vec():
        return pl.BlockSpec((1, D), lambda i: (0, 0))

    in_specs = [rows(hidden), rows(D),
                pl.BlockSpec((hidden, D), lambda i: (0, 0), pipeline_mode=pl.Buffered(1)), vec()]
    args = [hid, x, w_out, g_post]
    out_specs = [rows(D)]
    out_shape = [jax.ShapeDtypeStruct((T, D), F32)]
    if emit_next:
        in_specs.append(vec())
        args.append(g_next)
        out_specs.append(rows(D))
        out_shape.append(jax.ShapeDtypeStruct((T, D), BF16))
    outs = pl.pallas_call(
        functools.partial(_ffn_down_kernel, emit_next=emit_next),
        grid=(T // tm,),
        in_specs=in_specs,
        out_specs=out_specs,
        out_shape=out_shape,
        compiler_params=_compiler_params(("parallel",), FFN_DOWN_VMEM_LIMIT_BYTES),
        name="ffn_down",
    )(*args)
    return (outs[0], outs[1]) if emit_next else (outs[0], None)


def kernel(x, positions, g_mix_pre, g_mix_post, g_ffn_pre, g_ffn_post, w_in, pool_w, pool_scale, conv_dw,
           conv_b, conv_ln_g, conv_ln_b, ret_gn_g, w_pool_proj, w_conv_proj, w_ret_proj, w_out, w_ffn_in,
           w_ffn_out):
    B, S, D = x.shape
    depth = w_in.shape[0]
    T = B * S
    pool_width = pool_scale.shape[-1]
    conv_width = conv_b.shape[-1]
    ret_width = ret_gn_g.shape[-1]
    n_mix = pool_width + 2 * conv_width + 4 * ret_width
    hidden = w_ffn_out.shape[1]
    head_dim = ret_width // RET_HEADS
    chunk = RET_CHUNK

    tm_mix = _pick(S, (MIX_ROWS,))
    tm_gate = _pick(T, (2048, 1024, 512, 256, 128))
    tn_gate = _pick(N_BRANCH * D, (2048, 1024, 512, 256, 128))
    tm_merge = _pick(T, (512, 256, 128))
    tm_up = _pick(T, (2048, 1024, 512, 256, 128))
    tn_up = _pick(hidden, (512, 256, 128))
    tm_down = _pick(T, (512, 256, 128))

    xf = x.reshape(T, D)
    xn, cosf, sins, w_mix, w_gate, wfi16 = _prologue(xf, g_mix_pre[0].reshape(1, D), positions, head_dim, w_in,
                                                     n_mix, w_ffn_in)
    merge_weights = (w_pool_proj, w_conv_proj, w_ret_proj, w_out)
    for l in range(depth):
        br, wp16, wc16, wr16, wo16 = _mixers(
            xn, w_mix, l, cosf, sins, pool_w, pool_scale[l], conv_dw[l], conv_b[l], conv_ln_g[l], conv_ln_b[l],
            ret_gn_g[l], merge_weights, seq_len=S, tm=tm_mix, chunk=chunk)
        gates = _proj_gate(xn, w_gate, tm=tm_gate, tn=tn_gate)
        xf, xn = _merge(br, gates, xf, wp16, wc16, wr16, wo16, g_mix_post[l].reshape(1, D),
                        g_ffn_pre[l].reshape(1, D), tm=tm_merge)
        hid, wfo16, w_mix, w_gate, wfi16 = _ffn_up(xn, wfi16, l, w_ffn_out, w_in, n_mix, w_ffn_in,
                                                   tm=tm_up, tn=tn_up)
        g_next = g_mix_pre[l + 1].reshape(1, D) if l + 1 < depth else None
        xf, xn = _ffn_down(hid, xf, wfo16, g_ffn_post[l].reshape(1, D), g_next, tm=tm_down)
    return xf.reshape(B, S, D)
```

```python
import functools
import math

import jax
import jax.numpy as jnp
from jax import lax
from jax.experimental import pallas as pl
from jax.experimental.pallas import tpu as pltpu

F32 = jnp.float32
BF16 = jnp.bfloat16

NORM_EPS = 1e-6
LN_EPS = 1e-5
ROPE_BASE = 10000.0
POOL_WINDOWS = (2, 4, 8, 16)
RET_HEADS = 8
N_BRANCH = 3

LANES = 128
SUBLANES = 8
BF16_ROWS = 16
VMEM_LIMIT_BYTES = 56 * 1024 * 1024
MERGE_VMEM_LIMIT_BYTES = 59 * 1024 * 1024
FFN_DOWN_VMEM_LIMIT_BYTES = 60 * 1024 * 1024

POOL_HALO = 16
CONV_HALO = 32
CONV_ROWS = 64
RET_HEAD_GROUP = 4
GATE_ROWS = 512
MIX_ROWS = 512
RET_CHUNK = 256


def _rms_norm(x, g):
    return x * lax.rsqrt(jnp.mean(x * x, axis=-1, keepdims=True) + NORM_EPS) * g


def _sigmoid(x):
    return 0.5 * jnp.tanh(0.5 * x) + 0.5


def _compiler_params(semantics, vmem_limit_bytes=VMEM_LIMIT_BYTES):
    return pltpu.CompilerParams(dimension_semantics=semantics, vmem_limit_bytes=vmem_limit_bytes)


def _pick(n, candidates):
    for c in candidates:
        if n % c == 0:
            return c
    raise ValueError(f"no tile in {candidates} divides {n}")


def _rope(pos, half):
    lane = lax.broadcasted_iota(jnp.int32, (1, 2 * half), 1)
    freq = (lane % half).astype(F32)
    inv = jnp.exp(freq * (-math.log(ROPE_BASE) / half))
    ang = pos * inv
    sin = jnp.sin(ang)
    return jnp.cos(ang), jnp.where(lane < half, -sin, sin)


def _cast_specs(w, layer, n_steps, period=1):
    _, R, C = w.shape
    rows = R * period // n_steps
    assert R * period % n_steps == 0 and rows % BF16_ROWS == 0 and C % LANES == 0
    in_spec = pl.BlockSpec((None, rows, C), lambda i: (layer, i // period, 0))
    out_spec = pl.BlockSpec((rows, C), lambda i: (i // period, 0))
    return in_spec, out_spec, jax.ShapeDtypeStruct((R, C), BF16)


def _cast_period(w, n_steps):
    period = 1
    while (w.shape[1] * period) % (n_steps * BF16_ROWS):
        period *= 2
    return period


def _split_cast_specs(w, layer, n_steps, n_first):
    _, R, C = w.shape
    rows = R // n_steps
    assert R % n_steps == 0 and rows % BF16_ROWS == 0 and n_first % LANES == 0 and C % LANES == 0
    in_spec = pl.BlockSpec((None, rows, C), lambda i: (layer, i, 0))
    out_specs = [pl.BlockSpec((rows, n_first), lambda i: (i, 0)), pl.BlockSpec((rows, C - n_first), lambda i: (i, 0))]
    out_shapes = [jax.ShapeDtypeStruct((R, n_first), BF16), jax.ShapeDtypeStruct((R, C - n_first), BF16)]
    return in_spec, out_specs, out_shapes


def _grid2_cast_blocks(w, n_i, n_j):
    R = w.shape[1]
    for m in range(n_j, 0, -1):
        if R % (n_i * m) == 0 and (R // (n_i * m)) % BF16_ROWS == 0:
            return m, R // (n_i * m)
    raise ValueError(f"cannot split {R} rows over a ({n_i}, {n_j}) grid")


def _grid2_cast_specs(w, layer, n_i, n_j, n_first=None):
    _, R, C = w.shape
    m, rows = _grid2_cast_blocks(w, n_i, n_j)

    def block(i, j):
        return i * m + jnp.minimum(j, m - 1)

    in_spec = pl.BlockSpec((None, rows, C), lambda i, j: (layer, block(i, j), 0))
    widths = [C] if n_first is None else [n_first, C - n_first]
    assert all(width % LANES == 0 for width in widths)
    out_specs = [pl.BlockSpec((rows, width), lambda i, j: (block(i, j), 0)) for width in widths]
    out_shapes = [jax.ShapeDtypeStruct((R, width), BF16) for width in widths]
    return in_spec, out_specs, out_shapes


def _split_cast(w_ref, first_ref, second_ref):
    n_first = first_ref.shape[1]
    first_ref[...] = w_ref[:, 0:n_first].astype(first_ref.dtype)
    second_ref[...] = w_ref[:, n_first:].astype(second_ref.dtype)


def _prologue_kernel(x_ref, g_ref, pos_ref, w_ref, wfi_ref, o_ref, cos_ref, sin_ref, wmix_ref, wgate_ref,
                     wfi16_ref):
    o_ref[...] = _rms_norm(x_ref[...], g_ref[...]).astype(o_ref.dtype)
    cos_ref[...], sin_ref[...] = _rope(pos_ref[...], pos_ref.shape[1] // 2)
    _split_cast(w_ref, wmix_ref, wgate_ref)
    wfi16_ref[...] = wfi_ref[...].astype(wfi16_ref.dtype)


def _prologue(x, g, positions, head_dim, w_in, n_mix, w_ffn_in):
    T, D = x.shape
    tm = _pick(T, (512, 256, 128))
    posb = jnp.broadcast_to(positions.astype(F32).reshape(T, 1), (T, head_dim))
    w_spec, w_out_specs, w_out_shapes = _split_cast_specs(w_in, 0, T // tm, n_mix)
    wfi_spec, wfi_out_spec, wfi_shape = _cast_specs(w_ffn_in, 0, T // tm, _cast_period(w_ffn_in, T // tm))
    rows = pl.BlockSpec((tm, D), lambda i: (i, 0))
    rope = pl.BlockSpec((tm, head_dim), lambda i: (i, 0))
    return pl.pallas_call(
        _prologue_kernel,
        grid=(T // tm,),
        in_specs=[rows, pl.BlockSpec((1, D), lambda i: (0, 0)), rope, w_spec, wfi_spec],
        out_specs=[rows, rope, rope] + w_out_specs + [wfi_out_spec],
        out_shape=[jax.ShapeDtypeStruct((T, D), BF16)] + [jax.ShapeDtypeStruct((T, head_dim), F32)] * 2
        + w_out_shapes + [wfi_shape],
        compiler_params=_compiler_params(("parallel",)),
        name="prologue",
    )(x, g, posb, w_in, w_ffn_in)


def _proj_gate_kernel(x_ref, w_ref, o_ref, *, tn):
    for c0 in range(0, w_ref.shape[1], tn):
        acc = jnp.dot(x_ref[...], w_ref[:, c0:c0 + tn], preferred_element_type=F32)
        o_ref[:, c0:c0 + tn] = _sigmoid(acc).astype(o_ref.dtype)


def _proj_gate(xn, w, *, tm, tn):
    T, D = xn.shape
    N = w.shape[1]
    return pl.pallas_call(
        functools.partial(_proj_gate_kernel, tn=tn),
        grid=(T // tm,),
        in_specs=[
            pl.BlockSpec((tm, D), lambda i: (i, 0)),
            pl.BlockSpec((D, N), lambda i: (0, 0), pipeline_mode=pl.Buffered(1)),
        ],
        out_specs=pl.BlockSpec((tm, N), lambda i: (i, 0)),
        out_shape=jax.ShapeDtypeStruct((T, N), BF16),
        compiler_params=_compiler_params(("parallel",)),
        name="proj_gate",
    )(xn, w)


def _rotary(t, cos, sin):
    return t * cos + pltpu.roll(t, t.shape[1] // 2, axis=1) * sin


def _mixer_kernel(x_ref, w_ref, cos_ref, sin_ref, poolw_ref, pscale_ref, dw_ref, cb_ref, lng_ref, lnb_ref,
                  gng_ref, *rest, n_casts, tm, chunk, tiles_per_seq, pool_width, conv_width, conv_kernel,
                  ret_width):
    cast_in, out_ref, cast_out = rest[:n_casts], rest[n_casts], rest[n_casts + 1:2 * n_casts + 1]
    u_scr, h_scr, state_scr, decay_scr, xi_scr, zeta_scr = rest[2 * n_casts + 1:]
    for src_ref, dst_ref in zip(cast_in, cast_out):
        dst_ref[...] = src_ref[...].astype(dst_ref.dtype)

    tile = pl.program_id(0)
    s = tile % tiles_per_seq
    heads = RET_HEADS
    dh = ret_width // heads
    pool_group = pool_width // len(POOL_WINDOWS)
    log_gamma = [math.log1p(-(2.0 ** (-5.0 - h))) for h in range(heads)]

    off_a = pool_width
    off_g = off_a + conv_width
    off_q = off_g + conv_width
    off_k = off_q + ret_width
    off_v = off_k + ret_width
    off_gate = off_v + ret_width

    @pl.when(tile == 0)
    def _():
        row = lax.broadcasted_iota(jnp.int32, (chunk, chunk), 0).astype(F32)
        col = lax.broadcasted_iota(jnp.int32, (chunk, chunk), 1).astype(F32)
        rel = row - col
        pos = lax.broadcasted_iota(jnp.int32, (chunk, dh), 0).astype(F32)
        for h in range(heads):
            decay_scr[h] = jnp.where(rel >= 0, jnp.exp(jnp.maximum(rel, 0.0) * log_gamma[h]), 0.0)
            xi_scr[h] = jnp.exp((pos + 1.0) * log_gamma[h])
            zeta_scr[h] = jnp.exp((chunk - 1.0 - pos) * log_gamma[h])

    @pl.when(s == 0)
    def _():
        u_scr[0:POOL_HALO, :] = jnp.zeros((POOL_HALO, pool_width), F32)
        h_scr[0:CONV_HALO, :] = jnp.zeros((CONV_HALO, conv_width), F32)
        state_scr[...] = jnp.zeros_like(state_scr)

    for r_off in range(0, tm, MIX_ROWS):
        sub = slice(r_off, r_off + MIX_ROWS)

        def stream(c0, width, sub=sub):
            return jnp.dot(x_ref[sub, :], w_ref[:, c0:c0 + width], preferred_element_type=F32)

        u_scr[POOL_HALO + r_off:POOL_HALO + r_off + MIX_ROWS, :] = stream(0, pool_width)
        t_idx = s * tm + r_off + lax.broadcasted_iota(jnp.int32, (MIX_ROWS, 1), 0)
        for gi, w in enumerate(POOL_WINDOWS):
            c0 = gi * pool_group
            ug = u_scr[r_off:r_off + POOL_HALO + MIX_ROWS, c0:c0 + pool_group]
            win = ug
            span = 1
            while span < w:
                win = win + pltpu.roll(win, span, axis=0)
                span *= 2
            cnt = jnp.minimum(t_idx + 1, w).astype(F32)
            p = win[POOL_HALO:, :] / cnt - ug[POOL_HALO:, :]
            y = jnp.dot(p.astype(BF16), poolw_ref[gi].astype(BF16), preferred_element_type=F32)
            out_ref[sub, c0:c0 + pool_group] = (y * pscale_ref[:, c0:c0 + pool_group]).astype(out_ref.dtype)

        h_scr[CONV_HALO + r_off:CONV_HALO + r_off + MIX_ROWS, :] = (
            stream(off_a, conv_width) * _sigmoid(stream(off_g, conv_width)))
        win_rows = CONV_ROWS + SUBLANES
        for r0 in range(r_off, r_off + MIX_ROWS, CONV_ROWS):
            base = CONV_HALO + r0 - SUBLANES
            blocks = []
            for c0 in range(0, conv_width, LANES):
                cols = slice(c0, c0 + LANES)
                acc = None
                for r in range(SUBLANES):
                    part = None
                    for a in range((conv_kernel - 1 - r) // SUBLANES + 1):
                        j = conv_kernel - 1 - (SUBLANES * a + r)
                        lo = base - SUBLANES * a
                        term = dw_ref[j:j + 1, cols] * h_scr[lo:lo + win_rows, cols]
                        part = term if part is None else part + term
                    if r:
                        part = pltpu.roll(part, r, axis=0)
                    acc = part if acc is None else acc + part
                blocks.append(acc[SUBLANES:, :] + cb_ref[:, cols])
            conv = jnp.concatenate(blocks, axis=1)
            mu = jnp.mean(conv, axis=-1, keepdims=True)
            d = conv - mu
            var = jnp.mean(d * d, axis=-1, keepdims=True)
            y = d * lax.rsqrt(var + LN_EPS) * lng_ref[...] + lnb_ref[...]
            out_ref[r0:r0 + CONV_ROWS, pool_width:pool_width + conv_width] = (
                y * _sigmoid(y)).astype(out_ref.dtype)

        out_off = pool_width + conv_width
        cosf = cos_ref[sub, :]
        sins = sin_ref[sub, :]
        group = RET_HEAD_GROUP
        for h0 in range(0, heads, group):
            width = group * dh
            q_g = stream(off_q + h0 * dh, width)
            k_g = stream(off_k + h0 * dh, width)
            v_g = stream(off_v + h0 * dh, width)
            gate_g = stream(off_gate + h0 * dh, width)
            for hh in range(group):
                h = h0 + hh
                lanes = slice(hh * dh, (hh + 1) * dh)
                gamma_c = math.exp(chunk * log_gamma[h])
                qr = _rotary(q_g[:, lanes], cosf, sins).astype(BF16)
                kr = (_rotary(k_g[:, lanes], cosf, sins) * (dh ** -0.5)).astype(BF16)
                for c in range(MIX_ROWS // chunk):
                    rows = slice(c * chunk, (c + 1) * chunk)
                    q = qr[rows]
                    k = kr[rows]
                    v = v_g[rows, lanes]
                    scores = lax.dot_general(q, k, (((1,), (1,)), ((), ())), preferred_element_type=F32)
                    scores = scores * decay_scr[h]
                    intra = jnp.dot(scores.astype(BF16), v.astype(BF16), preferred_element_type=F32)
                    state = state_scr[h]
                    inter = jnp.dot(q, state.astype(BF16), preferred_element_type=F32) * xi_scr[h]
                    kv = lax.dot_general(k, (v * zeta_scr[h]).astype(BF16), (((0,), (0,)), ((), ())),
                                         preferred_element_type=F32)
                    state_scr[h] = gamma_c * state + kv
                    o = intra + inter
                    mu = jnp.mean(o, axis=-1, keepdims=True)
                    d = o - mu
                    var = jnp.mean(d * d, axis=-1, keepdims=True)
                    o = d * lax.rsqrt(var + LN_EPS) * gng_ref[:, h * dh:(h + 1) * dh]
                    gate = gate_g[rows, lanes]
                    out_rows = slice(r_off + c * chunk, r_off + (c + 1) * chunk)
                    out_ref[out_rows, out_off + h * dh:out_off + (h + 1) * dh] = (
                        gate * _sigmoid(gate) * o).astype(out_ref.dtype)

    u_scr[0:POOL_HALO, :] = u_scr[tm:tm + POOL_HALO, :]
    h_scr[0:CONV_HALO, :] = h_scr[tm:tm + CONV_HALO, :]


def _mixers(xn, w_mix, layer, cosf, sins, pool_w, pool_scale, conv_dw, conv_b, ln_g, ln_b, gn_g, cast_weights,
            *, seq_len, tm, chunk):
    T, d_model = xn.shape
    pool_width = pool_scale.shape[-1]
    conv_kernel, conv_width = conv_dw.shape
    ret_width = gn_g.shape[-1]
    n_mix = pool_width + 2 * conv_width + 4 * ret_width
    dh = ret_width // RET_HEADS
    n_groups = len(POOL_WINDOWS)
    pool_group = pool_width // n_groups
    n_steps = T // tm
    assert pool_width + conv_width + ret_width == d_model and w_mix.shape == (d_model, n_mix)
    assert dh == LANES and pool_group == LANES and conv_width % LANES == 0 and RET_HEADS % RET_HEAD_GROUP == 0
    assert chunk % SUBLANES == 0 and MIX_ROWS % chunk == 0 and MIX_ROWS % CONV_ROWS == 0
    assert tm % MIX_ROWS == 0 and seq_len % tm == 0
    assert max(POOL_WINDOWS) - 1 <= POOL_HALO and SUBLANES * ((conv_kernel - 1) // SUBLANES + 1) <= CONV_HALO

    def row(v):
        return v.reshape(1, -1)

    def const_spec(shape):
        return pl.BlockSpec(shape, lambda i: (0,) * len(shape))

    casts = [_cast_specs(w, layer, n_steps, _cast_period(w, n_steps)) for w in cast_weights]

    kern = functools.partial(_mixer_kernel, n_casts=len(casts), tm=tm, chunk=chunk, tiles_per_seq=seq_len // tm,
                             pool_width=pool_width, conv_width=conv_width, conv_kernel=conv_kernel,
                             ret_width=ret_width)
    return pl.pallas_call(
        kern,
        grid=(n_steps,),
        in_specs=[
            pl.BlockSpec((tm, d_model), lambda i: (i, 0)),
            pl.BlockSpec((d_model, n_mix), lambda i: (0, 0), pipeline_mode=pl.Buffered(1)),
            pl.BlockSpec((tm, dh), lambda i: (i, 0)),
            pl.BlockSpec((tm, dh), lambda i: (i, 0)),
            pl.BlockSpec((None, n_groups, pool_group, pool_group), lambda i: (layer, 0, 0, 0)),
            const_spec((1, pool_width)),
            const_spec((conv_kernel, conv_width)),
            const_spec((1, conv_width)),
            const_spec((1, conv_width)),
            const_spec((1, conv_width)),
            const_spec((1, ret_width)),
        ] + [c[0] for c in casts],
        out_specs=[pl.BlockSpec((tm, d_model), lambda i: (i, 0))] + [c[1] for c in casts],
        out_shape=[jax.ShapeDtypeStruct((T, d_model), BF16)] + [c[2] for c in casts],
        scratch_shapes=[
            pltpu.VMEM((POOL_HALO + tm, pool_width), F32),
            pltpu.VMEM((CONV_HALO + tm, conv_width), F32),
            pltpu.VMEM((RET_HEADS, dh, dh), F32),
            pltpu.VMEM((RET_HEADS, chunk, chunk), F32),
            pltpu.VMEM((RET_HEADS, chunk, dh), F32),
            pltpu.VMEM((RET_HEADS, chunk, dh), F32),
        ],
        compiler_params=_compiler_params(("arbitrary",)),
        name="mixers",
    )(xn, w_mix, cosf, sins, pool_w, row(pool_scale), conv_dw, row(conv_b), row(ln_g), row(ln_b), row(gn_g),
      *cast_weights)


def _merge_kernel(br_ref, gate_ref, x_ref, wp_ref, wc_ref, wr_ref, wo_ref, g_ref, gn_ref, o_ref, xn_ref,
                  *, pool_width, conv_width):
    d = x_ref.shape[-1]
    c1 = pool_width
    c2 = pool_width + conv_width
    y_pool = jnp.dot(br_ref[:, 0:c1], wp_ref[...], preferred_element_type=F32)
    merged = gate_ref[:, 0:d].astype(F32) * y_pool
    y_conv = jnp.dot(br_ref[:, c1:c2], wc_ref[...], preferred_element_type=F32)
    merged = merged + gate_ref[:, d:2 * d].astype(F32) * y_conv
    y_ret = jnp.dot(br_ref[:, c2:d], wr_ref[...], preferred_element_type=F32)
    merged = merged + gate_ref[:, 2 * d:3 * d].astype(F32) * y_ret
    out = jnp.dot(merged.astype(BF16), wo_ref[...], preferred_element_type=F32)
    x_new = x_ref[...] + _rms_norm(out, g_ref[...])
    o_ref[...] = x_new
    xn_ref[...] = _rms_norm(x_new, gn_ref[...]).astype(xn_ref.dtype)


def _merge(br, gates, x, wp, wc, wr, wo, g_post, g_next, *, tm):
    T, D = x.shape
    pool_width, conv_width, ret_width = wp.shape[0], wc.shape[0], wr.shape[0]

    def resident(rows):
        return pl.BlockSpec((rows, D), lambda i: (0, 0), pipeline_mode=pl.Buffered(1))

    def vec():
        return pl.BlockSpec((1, D), lambda i: (0, 0))

    def rows(width):
        return pl.BlockSpec((tm, width), lambda i: (i, 0))

    return pl.pallas_call(
        functools.partial(_merge_kernel, pool_width=pool_width, conv_width=conv_width),
        grid=(T // tm,),
        in_specs=[rows(D), rows(N_BRANCH * D), rows(D), resident(pool_width), resident(conv_width),
                  resident(ret_width), resident(D), vec(), vec()],
        out_specs=[rows(D), rows(D)],
        out_shape=[jax.ShapeDtypeStruct((T, D), F32), jax.ShapeDtypeStruct((T, D), BF16)],
        compiler_params=_compiler_params(("parallel",), MERGE_VMEM_LIMIT_BYTES),
        name="merge",
    )(br, gates, x, wp, wc, wr, wo, g_post, g_next)


def _ffn_up_kernel(x_ref, wa_ref, wb_ref, *rest, cast_next):
    if cast_next:
        wfo_ref, w_next_ref, wfi_next_ref, o_ref, wfo16_ref, wmix_ref, wgate_ref, wfi16_ref = rest
        _split_cast(w_next_ref, wmix_ref, wgate_ref)
        wfi16_ref[...] = wfi_next_ref[...].astype(wfi16_ref.dtype)
    else:
        wfo_ref, o_ref, wfo16_ref = rest
    wfo16_ref[...] = wfo_ref[...].astype(wfo16_ref.dtype)
    for r0 in range(0, x_ref.shape[0], GATE_ROWS):
        rows = slice(r0, min(r0 + GATE_ROWS, x_ref.shape[0]))
        xn = x_ref[rows, :]
        a = jnp.dot(xn, wa_ref[...], preferred_element_type=F32)
        b = jnp.dot(xn, wb_ref[...], preferred_element_type=F32)
        o_ref[rows, :] = (a * _sigmoid(a) * b).astype(o_ref.dtype)


def _ffn_up(xn, wfi16, layer, w_ffn_out, w_in, n_mix, w_ffn_in, *, tm, tn):
    T, D = xn.shape
    hidden = wfi16.shape[1] // 2
    nh = hidden // tn
    n_i = T // tm
    cast_next = layer + 1 < w_in.shape[0]
    in_specs = [
        pl.BlockSpec((tm, D), lambda i, j: (i, 0)),
        pl.BlockSpec((D, tn), lambda i, j: (0, j)),
        pl.BlockSpec((D, tn), lambda i, j: (0, j + nh)),
    ]
    args = [xn, wfi16, wfi16]
    out_specs = [pl.BlockSpec((tm, tn), lambda i, j: (i, j))]
    out_shape = [jax.ShapeDtypeStruct((T, hidden), BF16)]
    casts = [(w_ffn_out, layer, None)]
    if cast_next:
        casts += [(w_in, layer + 1, n_mix), (w_ffn_in, layer + 1, None)]
    for w, lyr, n_first in casts:
        c_in, c_outs, c_shapes = _grid2_cast_specs(w, lyr, n_i, nh, n_first)
        in_specs.append(c_in)
        args.append(w)
        out_specs += c_outs
        out_shape += c_shapes
    outs = pl.pallas_call(
        functools.partial(_ffn_up_kernel, cast_next=cast_next),
        grid=(n_i, nh),
        in_specs=in_specs,
        out_specs=out_specs,
        out_shape=out_shape,
        compiler_params=_compiler_params(("arbitrary", "arbitrary")),
        name="ffn_up",
    )(*args)
    return tuple(outs) if cast_next else (outs[0], outs[1], None, None, None)


def _ffn_down_kernel(h_ref, x_ref, wo_ref, gpost_ref, *rest, emit_next):
    if emit_next:
        gnext_ref, o_ref, xn_out_ref = rest
    else:
        (o_ref,) = rest
    o_ref[...] = jnp.dot(h_ref[...], wo_ref[...], preferred_element_type=F32)
    x_new = x_ref[...] + _rms_norm(o_ref[...], gpost_ref[...])
    o_ref[...] = x_new
    if emit_next:
        xn_out_ref[...] = _rms_norm(x_new, gnext_ref[...]).astype(xn_out_ref.dtype)


def _ffn_down(hid, x, w_out, g_post, g_next, *, tm):
    T, D = x.shape
    hidden = w_out.shape[0]
    emit_next = g_next is not None

    def rows(width):
        return pl.BlockSpec((tm, width), lambda i: (i, 0))

    def vec():
        return pl.BlockSpec((1, D), lambda i: (0, 0))

    in_specs = [rows(hidden), rows(D),
                pl.BlockSpec((hidden, D), lambda i: (0, 0), pipeline_mode=pl.Buffered(1)), vec()]
    args = [hid, x, w_out, g_post]
    out_specs = [rows(D)]
    out_shape = [jax.ShapeDtypeStruct((T, D), F32)]
    if emit_next:
        in_specs.append(vec())
        args.append(g_next)
        out_specs.append(rows(D))
        out_shape.append(jax.ShapeDtypeStruct((T, D), BF16))
    outs = pl.pallas_call(
        functools.partial(_ffn_down_kernel, emit_next=emit_next),
        grid=(T // tm,),
        in_specs=in_specs,
        out_specs=out_specs,
        out_shape=out_shape,
        compiler_params=_compiler_params(("parallel",), FFN_DOWN_VMEM_LIMIT_BYTES),
        name="ffn_down",
    )(*args)
    return (outs[0], outs[1]) if emit_next else (outs[0], None)


def kernel(x, positions, g_mix_pre, g_mix_post, g_ffn_pre, g_ffn_post, w_in, pool_w, pool_scale, conv_dw,
           conv_b, conv_ln_g, conv_ln_b, ret_gn_g, w_pool_proj, w_conv_proj, w_ret_proj, w_out, w_ffn_in,
           w_ffn_out):
    B, S, D = x.shape
    depth = w_in.shape[0]
    T = B * S
    pool_width = pool_scale.shape[-1]
    conv_width = conv_b.shape[-1]
    ret_width = ret_gn_g.shape[-1]
    n_mix = pool_width + 2 * conv_width + 4 * ret_width
    hidden = w_ffn_out.shape[1]
    head_dim = ret_width // RET_HEADS
    chunk = RET_CHUNK

    tm_mix = _pick(S, (MIX_ROWS,))
    tm_gate = _pick(T, (512, 256, 128))
    tn_gate = _pick(N_BRANCH * D, (1024, 512, 256, 128))
    tm_merge = _pick(T, (512, 256, 128))
    tm_up = _pick(T, (2048, 1024, 512, 256, 128))
    tn_up = _pick(hidden, (512, 256, 128))
    tm_down = _pick(T, (512, 256, 128))

    xf = x.reshape(T, D)
    xn, cosf, sins, w_mix, w_gate, wfi16 = _prologue(xf, g_mix_pre[0].reshape(1, D), positions, head_dim, w_in,
                                                     n_mix, w_ffn_in)
    merge_weights = (w_pool_proj, w_conv_proj, w_ret_proj, w_out)
    for l in range(depth):
        br, wp16, wc16, wr16, wo16 = _mixers(
            xn, w_mix, l, cosf, sins, pool_w, pool_scale[l], conv_dw[l], conv_b[l], conv_ln_g[l], conv_ln_b[l],
            ret_gn_g[l], merge_weights, seq_len=S, tm=tm_mix, chunk=chunk)
        gates = _proj_gate(xn, w_gate, tm=tm_gate, tn=tn_gate)
        xf, xn = _merge(br, gates, xf, wp16, wc16, wr16, wo16, g_mix_post[l].reshape(1, D),
                        g_ffn_pre[l].reshape(1, D), tm=tm_merge)
        hid, wfo16, w_mix, w_gate, wfi16 = _ffn_up(xn, wfi16, l, w_ffn_out, w_in, n_mix, w_ffn_in,
                                                   tm=tm_up, tn=tn_up)
        g_next = g_mix_pre[l + 1].reshape(1, D) if l + 1 < depth else None
        xf, xn = _ffn_down(hid, xf, wfo16, g_ffn_post[l].reshape(1, D), g_next, tm=tm_down)
    return xf.reshape(B, S, D)
```
